```python
import math
import jax, jax.numpy as jnp
from jax import lax
import numpy as np


D_MODEL = 1024
BATCH = 4
SEQ = 4096
DEPTH = 2

N_MIXERS = 2
MLA_HEADS = 8
MLA_Q_LORA = 384
MLA_KV_LORA = 256
MLA_NOPE = 128
MLA_ROPE = 64
MLA_V = 128
MLA_THETA = 10000.0
MLA_IN = MLA_Q_LORA + MLA_KV_LORA + MLA_ROPE
DIFF_HEADS = 8
DIFF_HEAD_DIM = 64
DIFF_V = 2 * DIFF_HEAD_DIM
DIFF_ROT = DIFF_HEAD_DIM // 4
ROPE_THETA = 500000.0
DIFF_IN = 4 * DIFF_HEADS * DIFF_HEAD_DIM + DIFF_HEADS * DIFF_V
N_EXPERTS = 16
EXPERT_FF = 2048
EC_CAPACITY_FACTOR = 2
Q_BLOCK = 128
NORM_EPS = 1e-5
LATENT_EPS = 1e-6
ALPHA = (2 * DEPTH) ** 0.25
BETA = (8 * DEPTH) ** -0.25
N_MLA_LAYERS = (DEPTH + 1) // 2
N_DIFF_LAYERS = DEPTH // 2

kernel_name = 'hybrid_mla_diffattn_ec_moe_encoder'


def layer_norm(x, g, b):
    xf = x.astype(jnp.float32)
    xc = xf - xf.mean(-1, keepdims=True)
    var = (xc * xc).mean(-1, keepdims=True)
    return (xc * lax.rsqrt(var + NORM_EPS)).astype(x.dtype) * g + b


def rms_norm(x, g, eps):
    xf = x.astype(jnp.float32)
    return (xf * lax.rsqrt((xf * xf).mean(-1, keepdims=True) + eps)).astype(x.dtype) * g


def rope_cos_sin(positions, rot_dim, theta):
    inv = theta ** (-jnp.arange(0, rot_dim, 2, dtype=jnp.float32) / rot_dim)
    ang = positions.astype(jnp.float32)[..., None] * inv
    return jnp.cos(ang), jnp.sin(ang)


def apply_rope(x, cos, sin):
    shape = cos.shape[:2] + (1,) * (x.ndim - 3) + cos.shape[-1:]
    c = cos.reshape(shape).astype(x.dtype)
    s = sin.reshape(shape).astype(x.dtype)
    half = x.shape[-1] // 2
    x1, x2 = x[..., :half], x[..., half:]
    return jnp.concatenate([x1 * c - x2 * s, x2 * c + x1 * s], -1)


def mla_mixer(h, cos, sin, w_in, q_norm, kv_norm, w_uq, w_ukv, w_o):
    B, S, _ = h.shape
    H = MLA_HEADS
    cq, ckv, kr = jnp.split(h @ w_in, [MLA_Q_LORA, MLA_Q_LORA + MLA_KV_LORA], -1)
    q = (rms_norm(cq, q_norm, LATENT_EPS) @ w_uq).reshape(B, S, H, MLA_NOPE + MLA_ROPE)
    q = jnp.concatenate([q[..., :MLA_NOPE], apply_rope(q[..., MLA_NOPE:], cos, sin)], -1)
    kv = (rms_norm(ckv, kv_norm, LATENT_EPS) @ w_ukv).reshape(B, S, H, MLA_NOPE + MLA_V)
    k_pe = apply_rope(kr[:, :, None, :], cos, sin)
    k = jnp.concatenate([kv[..., :MLA_NOPE], jnp.broadcast_to(k_pe, (B, S, H, MLA_ROPE))], -1)
    v = kv[..., MLA_NOPE:]
    scale = (MLA_NOPE + MLA_ROPE) ** -0.5
    nb = S // Q_BLOCK
    qb = q.reshape(B, nb, Q_BLOCK, H, MLA_NOPE + MLA_ROPE).transpose(1, 0, 2, 3, 4)

    def block(qi):
        s = jnp.einsum('bqhd,bkhd->bhqk', qi, k).astype(jnp.float32) * scale
        p = jax.nn.softmax(s, -1).astype(v.dtype)
        return jnp.einsum('bhqk,bkhd->bqhd', p, v)

    o = lax.map(block, qb).transpose(1, 0, 2, 3, 4).reshape(B, S, H * MLA_V)
    return o @ w_o


def diff_mixer(h, cos, sin, w_in, lam_vec, subln, w_o, lambda_init):
    B, S, _ = h.shape
    H, d = DIFF_HEADS, DIFF_HEAD_DIM
    qd = H * 2 * d
    q, k, v = jnp.split(h @ w_in, [qd, 2 * qd], -1)
    q = q.reshape(B, S, H, 2, d)
    k = k.reshape(B, S, H, 2, d)
    v = v.reshape(B, S, H, DIFF_V)

    def partial_rope(t):
        return jnp.concatenate([apply_rope(t[..., :DIFF_ROT], cos, sin), t[..., DIFF_ROT:]], -1)

    q, k = partial_rope(q), partial_rope(k)
    lf = lam_vec.astype(jnp.float32)
    lam = jnp.exp(jnp.sum(lf[0] * lf[1])) - jnp.exp(jnp.sum(lf[2] * lf[3])) + lambda_init
    scale = d ** -0.5
    nb = S // Q_BLOCK
    qb = q.reshape(B, nb, Q_BLOCK, H, 2, d).transpose(1, 0, 2, 3, 4, 5)

    def block(qi):
        s = jnp.einsum('bqhcd,bkhcd->bhcqk', qi, k).astype(jnp.float32) * scale
        p = jax.nn.softmax(s, -1)
        a = (p[:, :, 0] - lam * p[:, :, 1]).astype(v.dtype)
        return jnp.einsum('bhqk,bkhe->bqhe', a, v)

    o = lax.map(block, qb).transpose(1, 0, 2, 3, 4)
    o = rms_norm(o, subln, NORM_EPS) * (1.0 - lambda_init)
    return o.reshape(B, S, H * DIFF_V) @ w_o


def ec_moe(h, router_w, w_in, w_out):
    B, S, D = h.shape
    cap = max(1, EC_CAPACITY_FACTOR * S // N_EXPERTS)
    aff = jax.nn.softmax((h @ router_w).astype(jnp.float32), -1)
    gate, idx = lax.top_k(aff.transpose(0, 2, 1), cap)
    xe = jax.vmap(lambda hb, ib: hb[ib])(h, idx)
    gu = jnp.einsum('becd,edf->becf', xe, w_in)
    g, u = jnp.split(gu, 2, -1)
    y = jnp.einsum('becf,efd->becd', jax.nn.silu(g) * u, w_out) * gate[..., None].astype(h.dtype)
    return jax.vmap(
        lambda yb, ib: jnp.zeros((S, D), yb.dtype).at[ib.reshape(-1)].add(yb.reshape(-1, D))
    )(y, idx)


def setup_inputs(seed: int = 0) -> dict:
    key = jax.random.key(seed)
    ks = jax.random.split(key, 24)
    f32 = jnp.float32
    D = D_MODEL

    def nrm(k, shape, scale):
        return jax.random.normal(k, shape, f32) * scale

    x = jax.random.normal(ks[0], (BATCH, SEQ, D), f32)
    c = jax.random.normal(ks[1], (BATCH, D), f32)
    positions = (jnp.arange(SEQ, dtype=jnp.int32)[None, :]
                 + jax.random.randint(ks[2], (BATCH, 1), 0, SEQ, dtype=jnp.int32))
    ada_w = nrm(ks[3], (DEPTH, D, 6 * D), 0.1 * D ** -0.5)
    ada_b = nrm(ks[4], (DEPTH, 6 * D), 0.01)
    ln1_g = 1.0 + nrm(ks[5], (DEPTH, D), 0.02)
    ln1_b = nrm(ks[6], (DEPTH, D), 0.01)
    ln2_g = 1.0 + nrm(ks[7], (DEPTH, D), 0.02)
    ln2_b = nrm(ks[8], (DEPTH, D), 0.01)
    mla_w_in = nrm(ks[9], (N_MLA_LAYERS, D, MLA_IN), D ** -0.5)
    mla_q_norm = 1.0 + nrm(ks[10], (N_MLA_LAYERS, MLA_Q_LORA), 0.02)
    mla_kv_norm = 1.0 + nrm(ks[11], (N_MLA_LAYERS, MLA_KV_LORA), 0.02)
    mla_w_uq = nrm(ks[12], (N_MLA_LAYERS, MLA_Q_LORA, MLA_HEADS * (MLA_NOPE + MLA_ROPE)), MLA_Q_LORA ** -0.5)
    mla_w_ukv = nrm(ks[13], (N_MLA_LAYERS, MLA_KV_LORA, MLA_HEADS * (MLA_NOPE + MLA_V)), MLA_KV_LORA ** -0.5)
    mla_w_o = nrm(ks[14], (N_MLA_LAYERS, MLA_HEADS * MLA_V, D), BETA * (MLA_HEADS * MLA_V) ** -0.5)
    diff_w_in = nrm(ks[15], (N_DIFF_LAYERS, D, DIFF_IN), D ** -0.5)
    diff_lambda = nrm(ks[16], (N_DIFF_LAYERS, 4, DIFF_HEAD_DIM), 0.1)
    diff_subln = 1.0 + nrm(ks[17], (N_DIFF_LAYERS, DIFF_V), 0.02)
    diff_w_o = nrm(ks[18], (N_DIFF_LAYERS, DIFF_HEADS * DIFF_V, D), BETA * (DIFF_HEADS * DIFF_V) ** -0.5)
    router_w = nrm(ks[19], (DEPTH, D, N_EXPERTS), D ** -0.5)
    moe_w_in = nrm(ks[20], (DEPTH, N_EXPERTS, D, 2 * EXPERT_FF), D ** -0.5)
    moe_w_out = nrm(ks[21], (DEPTH, N_EXPERTS, EXPERT_FF, D), BETA * EXPERT_FF ** -0.5)
    return {'x': x, 'c': c, 'positions': positions, 'ada_w': ada_w, 'ada_b': ada_b,
            'ln1_g': ln1_g, 'ln1_b': ln1_b, 'ln2_g': ln2_g, 'ln2_b': ln2_b,
            'mla_w_in': mla_w_in, 'mla_q_norm': mla_q_norm, 'mla_kv_norm': mla_kv_norm,
            'mla_w_uq': mla_w_uq, 'mla_w_ukv': mla_w_ukv, 'mla_w_o': mla_w_o,
            'diff_w_in': diff_w_in, 'diff_lambda': diff_lambda, 'diff_subln': diff_subln,
            'diff_w_o': diff_w_o, 'router_w': router_w, 'moe_w_in': moe_w_in, 'moe_w_out': moe_w_out}


def reference(x, c, positions, ada_w, ada_b, ln1_g, ln1_b, ln2_g, ln2_b,
              mla_w_in, mla_q_norm, mla_kv_norm, mla_w_uq, mla_w_ukv, mla_w_o,
              diff_w_in, diff_lambda, diff_subln, diff_w_o,
              router_w, moe_w_in, moe_w_out):
    cos_m, sin_m = rope_cos_sin(positions, MLA_ROPE, MLA_THETA)
    cos_d, sin_d = rope_cos_sin(positions, DIFF_ROT, ROPE_THETA)
    mods = jnp.einsum('bd,lde->lbe', jax.nn.silu(c), ada_w) + ada_b[:, None, :]
    for i in range(DEPTH):
        sh1, sc1, g1, sh2, sc2, g2 = jnp.split(mods[i][:, None, :], 6, -1)
        h = x * (1.0 + sc1) + sh1
        j = i // N_MIXERS
        if i % N_MIXERS == 0:
            t = mla_mixer(h, cos_m, sin_m, mla_w_in[j], mla_q_norm[j], mla_kv_norm[j],
                          mla_w_uq[j], mla_w_ukv[j], mla_w_o[j])
        else:
            lambda_init = 0.8 - 0.6 * math.exp(-0.3 * i)
            t = diff_mixer(h, cos_d, sin_d, diff_w_in[j], diff_lambda[j], diff_subln[j],
                           diff_w_o[j], lambda_init)
        x = layer_norm(ALPHA * x + (1.0 + g1) * t, ln1_g[i], ln1_b[i])
        h = x * (1.0 + sc2) + sh2
        f = ec_moe(h, router_w[i], moe_w_in[i], moe_w_out[i])
        x = layer_norm(ALPHA * x + (1.0 + g2) * f, ln2_g[i], ln2_b[i])
    return x
```

```python
import functools
import math

import jax
import jax.numpy as jnp
from jax import lax
from jax.experimental import pallas as pl
from jax.experimental.pallas import tpu as pltpu

F32 = jnp.float32
BF16 = jnp.bfloat16

D_MODEL = 1024
DEPTH = 2
MLA_HEADS = 8
MLA_Q_LORA = 384
MLA_KV_LORA = 256
MLA_NOPE = 128
MLA_ROPE = 64
MLA_V = 128
MLA_THETA = 10000.0
DIFF_HEADS = 8
DIFF_HEAD_DIM = 64
DIFF_V = 2 * DIFF_HEAD_DIM
DIFF_ROT = DIFF_HEAD_DIM // 4
ROPE_THETA = 500000.0
N_EXPERTS = 16
EXPERT_FF = 2048
EC_CAPACITY_FACTOR = 2
NORM_EPS = 1e-5
LATENT_EPS = 1e-6
ALPHA = (2 * DEPTH) ** 0.25

LANES = 128
VMEM_LIMIT = 56 * 1024 * 1024

TOKEN_TILE = 512
Q_TILE = 256
FF_TILE = 256
COMBINE_TILE = 256


def _cparams(*sem):
    return pltpu.CompilerParams(dimension_semantics=sem, vmem_limit_bytes=VMEM_LIMIT)


def _dot(a, b):
    return jnp.dot(a, b, preferred_element_type=F32)


def _split_bf16(a):
    hi = a.astype(BF16)
    lo = (a - hi.astype(F32)).astype(BF16)
    return hi, lo


def _dot3(a, b):
    ah, al = _split_bf16(a)
    bh, bl = _split_bf16(b)
    return _dot(ah, bh) + (_dot(ah, bl) + _dot(al, bh))


def _layer_norm(y, g, b):
    mu = jnp.mean(y, axis=-1, keepdims=True)
    yc = y - mu
    var = jnp.mean(yc * yc, axis=-1, keepdims=True)
    return yc * lax.rsqrt(var + NORM_EPS) * g + b


def _mods_kernel(c_ref, w_ref, b_ref, o_ref):
    c = c_ref[...]
    a = c * (1.0 / (1.0 + jnp.exp(-c)))
    o_ref[0] = _dot3(a, w_ref[0]) + b_ref[0]


def _mods(c, ada_w, ada_b):
    B, D = c.shape
    L, _, N = ada_w.shape
    rows = 8
    cp = jnp.zeros((rows, D), F32).at[:B].set(c)
    tn = 1536
    return pl.pallas_call(
        _mods_kernel,
        grid=(L, N // tn),
        in_specs=[
            pl.BlockSpec((rows, D), lambda l, j: (0, 0)),
            pl.BlockSpec((1, D, tn), lambda l, j: (l, 0, j)),
            pl.BlockSpec((1, 1, tn), lambda l, j: (l, 0, j)),
        ],
        out_specs=pl.BlockSpec((1, rows, tn), lambda l, j: (l, 0, j)),
        out_shape=jax.ShapeDtypeStruct((L, rows, N), F32),
        compiler_params=_cparams("parallel", "parallel"),
        name="adaln_mods",
    )(cp, ada_w, ada_b.reshape(L, 1, N))


def _mla_proj_kernel(x_ref, mod_ref, cos_ref, sin_ref, win_ref, qn_ref, kvn_ref, wq_ref, wkv_ref,
                     q_ref, k_ref, v_ref, *, scale):
    m = mod_ref[0]
    h = (x_ref[...] * (1.0 + m[1:2]) + m[0:1]).astype(BF16)
    z = _dot(h, win_ref[...])
    cq = z[:, :MLA_Q_LORA]
    ckv = z[:, MLA_Q_LORA:MLA_Q_LORA + MLA_KV_LORA]
    kr = z[:, 640:768]
    krs = z[:, 768:896]
    cos = cos_ref[...]
    sin = sin_ref[...]
    qn = (cq * lax.rsqrt(jnp.mean(cq * cq, axis=-1, keepdims=True) + LATENT_EPS) * qn_ref[...]).astype(BF16)
    kvn = (ckv * lax.rsqrt(jnp.mean(ckv * ckv, axis=-1, keepdims=True) + LATENT_EPS) * kvn_ref[...]).astype(BF16)
    q = _dot(qn, wq_ref[...])
    kv = _dot(kvn, wkv_ref[...])
    kpe = (kr * cos + krs * sin).astype(BF16)
    hw = MLA_HEADS * LANES
    for hh in range(MLA_HEADS):
        a, b = hh * LANES, (hh + 1) * LANES
        q_ref[0, hh, :, 0:LANES] = (q[:, a:b] * scale).astype(BF16)
        q_ref[0, hh, :, LANES:2 * LANES] = (
            (q[:, hw + a:hw + b] * cos + q[:, 2 * hw + a:2 * hw + b] * sin) * scale).astype(BF16)
        k_ref[0, hh, :, 0:LANES] = kv[:, a:b].astype(BF16)
        k_ref[0, hh, :, LANES:2 * LANES] = kpe
        v_ref[0, hh] = kv[:, hw + a:hw + b].astype(BF16)


def _mla_proj(x2, mods3, cos_t, sin_t, w_in, q_norm, kv_norm, w_uq, w_ukv, B, S):
    T, D = x2.shape
    H = MLA_HEADS
    tm = min(TOKEN_TILE, S)
    npb = S // tm
    half = MLA_ROPE // 2
    kr = w_in[:, 640:704]
    kr_sw = jnp.concatenate([-kr[:, half:], kr[:, :half]], axis=1)
    z64 = jnp.zeros((D, 64), F32)
    win_p = jnp.concatenate([w_in[:, :640], kr, z64, kr_sw, z64], axis=1).astype(BF16)
    wq3 = w_uq.reshape(MLA_Q_LORA, H, MLA_NOPE + MLA_ROPE)
    rope = wq3[:, :, MLA_NOPE:]
    rope_sw = jnp.concatenate([-rope[:, :, half:], rope[:, :, :half]], axis=2)
    pad = lambda t: jnp.pad(t, ((0, 0), (0, 0), (0, LANES - MLA_ROPE))).reshape(MLA_Q_LORA, H * LANES)
    wq_p = jnp.concatenate([wq3[:, :, :MLA_NOPE].reshape(MLA_Q_LORA, H * MLA_NOPE), pad(rope), pad(rope_sw)],
                           axis=1).astype(BF16)
    wkv3 = w_ukv.reshape(MLA_KV_LORA, H, MLA_NOPE + MLA_V)
    wkv_p = jnp.concatenate([wkv3[:, :, :MLA_NOPE].reshape(MLA_KV_LORA, H * MLA_NOPE),
                             wkv3[:, :, MLA_NOPE:].reshape(MLA_KV_LORA, H * MLA_V)], axis=1).astype(BF16)
    scale = (MLA_NOPE + MLA_ROPE) ** -0.5
    full = lambda shape: pl.BlockSpec(shape, lambda i: (0,) * len(shape))
    return pl.pallas_call(
        functools.partial(_mla_proj_kernel, scale=scale),
        grid=(T // tm,),
        in_specs=[
            pl.BlockSpec((tm, D), lambda i: (i, 0)),
            pl.BlockSpec((1, 6, D), lambda i: (i // npb, 0, 0)),
            pl.BlockSpec((tm, LANES), lambda i: (i, 0)),
            pl.BlockSpec((tm, LANES), lambda i: (i, 0)),
            full(win_p.shape),
            full((1, MLA_Q_LORA)),
            full((1, MLA_KV_LORA)),
            full(wq_p.shape),
            full(wkv_p.shape),
        ],
        out_specs=[
            pl.BlockSpec((1, H, tm, 2 * LANES), lambda i: (i // npb, 0, i % npb, 0)),
            pl.BlockSpec((1, H, tm, 2 * LANES), lambda i: (i // npb, 0, i % npb, 0)),
            pl.BlockSpec((1, H, tm, MLA_V), lambda i: (i // npb, 0, i % npb, 0)),
        ],
        out_shape=[
            jax.ShapeDtypeStruct((B, H, S, 2 * LANES), BF16),
            jax.ShapeDtypeStruct((B, H, S, 2 * LANES), BF16),
            jax.ShapeDtypeStruct((B, H, S, MLA_V), BF16),
        ],
        compiler_params=_cparams("parallel"),
        name="mla_proj",
    )(x2, mods3, cos_t, sin_t, win_p, q_norm.reshape(1, -1), kv_norm.reshape(1, -1), wq_p, wkv_p)


_NT = (((1,), (1,)), ((), ()))


def _mla_attn_kernel(q_ref, k_ref, v_ref, o_ref):
    s = lax.dot_general(q_ref[0, 0], k_ref[0, 0], _NT, preferred_element_type=F32)
    m = jnp.max(s, axis=-1, keepdims=True)
    e = jnp.exp(s - m)
    l = jnp.sum(e, axis=-1, keepdims=True)
    o = _dot(e.astype(BF16), v_ref[0, 0])
    o_ref[0] = (o * (1.0 / l)).astype(BF16)


def _diff_attn_kernel(lam_ref, subln_ref, q_ref, k_ref, v_ref, o_ref, *, lambda_init):
    q = q_ref[0, 0]
    tq = q.shape[0]
    lane = lax.broadcasted_iota(jnp.int32, q.shape, 1)
    zero = jnp.zeros_like(q)
    qs = jnp.concatenate([jnp.where(lane < DIFF_HEAD_DIM, q, zero),
                          jnp.where(lane >= DIFF_HEAD_DIM, q, zero)], axis=0)
    s = lax.dot_general(qs, k_ref[0, 0], _NT, preferred_element_type=F32)
    m = jnp.max(s, axis=-1, keepdims=True)
    e = jnp.exp(s - m)
    r = 1.0 / jnp.sum(e, axis=-1, keepdims=True)
    lf = lam_ref[...]
    lam = (jnp.exp(jnp.sum(lf[0:1] * lf[1:2], axis=-1, keepdims=True))
           - jnp.exp(jnp.sum(lf[2:3] * lf[3:4], axis=-1, keepdims=True)) + lambda_init)
    a = e[:tq] * r[:tq] - e[tq:] * (lam * r[tq:])
    o = _dot(a.astype(BF16), v_ref[0, 0])
    o = o * lax.rsqrt(jnp.mean(o * o, axis=-1, keepdims=True) + NORM_EPS) * subln_ref[...]
    o_ref[0] = (o * (1.0 - lambda_init)).astype(BF16)


def _attention(kernel_fn, extra, q, k, v, name):
    B, H, S, dq = q.shape
    dv = v.shape[-1]
    tq = min(Q_TILE, S)
    extra_specs = [pl.BlockSpec(a.shape, lambda b, h, i, n=a.ndim: (0,) * n) for a in extra]
    return pl.pallas_call(
        kernel_fn,
        grid=(B, H, S // tq),
        in_specs=extra_specs + [
            pl.BlockSpec((1, 1, tq, dq), lambda b, h, i: (b, h, i, 0)),
            pl.BlockSpec((1, 1, S, dq), lambda b, h, i: (b, h, 0, 0)),
            pl.BlockSpec((1, 1, S, dv), lambda b, h, i: (b, h, 0, 0)),
        ],
        out_specs=pl.BlockSpec((1, tq, dv), lambda b, h, i: (b, i, h)),
        out_shape=jax.ShapeDtypeStruct((B, S, H * dv), BF16),
        compiler_params=_cparams("parallel", "parallel", "parallel"),
        name=name,
    )(*extra, q, k, v)


def _diff_proj_kernel(x_ref, mod_ref, c_ref, sa_ref, sb_ref, win_ref, q_ref, k_ref, v_ref, *, scale):
    m = mod_ref[0]
    h = (x_ref[...] * (1.0 + m[1:2]) + m[0:1]).astype(BF16)
    z = _dot(h, win_ref[...])
    c = c_ref[...]
    sa = sa_ref[...]
    sb = sb_ref[...]
    hw = DIFF_HEADS * LANES

    def rope(t):
        return t * c + pltpu.roll(t, LANES - DIFF_ROT // 2, 1) * sa + pltpu.roll(t, DIFF_ROT // 2, 1) * sb

    for hh in range(DIFF_HEADS):
        a, b = hh * LANES, (hh + 1) * LANES
        q_ref[0, hh] = (rope(z[:, a:b]) * scale).astype(BF16)
        k_ref[0, hh] = rope(z[:, hw + a:hw + b]).astype(BF16)
        v_ref[0, hh] = z[:, 2 * hw + a:2 * hw + b].astype(BF16)


def _diff_proj(x2, mods3, c_t, sa_t, sb_t, w_in, B, S):
    T, D = x2.shape
    H = DIFF_HEADS
    tm = min(TOKEN_TILE, S)
    npb = S // tm
    win = w_in.astype(BF16)
    hd = jax.ShapeDtypeStruct((B, H, S, LANES), BF16)
    hspec = pl.BlockSpec((1, H, tm, LANES), lambda i: (i // npb, 0, i % npb, 0))
    return pl.pallas_call(
        functools.partial(_diff_proj_kernel, scale=DIFF_HEAD_DIM ** -0.5),
        grid=(T // tm,),
        in_specs=[
            pl.BlockSpec((tm, D), lambda i: (i, 0)),
            pl.BlockSpec((1, 6, D), lambda i: (i // npb, 0, 0)),
            pl.BlockSpec((tm, LANES), lambda i: (i, 0)),
            pl.BlockSpec((tm, LANES), lambda i: (i, 0)),
            pl.BlockSpec((tm, LANES), lambda i: (i, 0)),
            pl.BlockSpec(win.shape, lambda i: (0, 0)),
        ],
        out_specs=[hspec, hspec, hspec],
        out_shape=[hd, hd, hd],
        compiler_params=_cparams("parallel"),
        name="diff_proj",
    )(x2, mods3, c_t, sa_t, sb_t, win)


def _post_attn_kernel(o_ref, x_ref, mod_ref, wo_ref, g_ref, b_ref, rw_ref, x1_ref, h2_ref, aff_ref):
    m = mod_ref[0]
    t = _dot(o_ref[...], wo_ref[...])
    x1 = _layer_norm(ALPHA * x_ref[...] + (1.0 + m[2:3]) * t, g_ref[...], b_ref[...])
    x1_ref[...] = x1
    h2 = x1 * (1.0 + m[4:5]) + m[3:4]
    h2_ref[...] = h2.astype(BF16)
    logits = _dot3(h2, rw_ref[...])
    lt = logits.T[:N_EXPERTS]
    mx = jnp.max(lt, axis=0, keepdims=True)
    ex = jnp.exp(lt - mx)
    aff_ref[0] = ex / jnp.sum(ex, axis=0, keepdims=True)


def _post_attn(o2, x2, mods3, w_o, ln_g, ln_b, router_w, B, S):
    T, D = x2.shape
    E = N_EXPERTS
    tm = min(TOKEN_TILE, S)
    npb = S // tm
    rw = jnp.pad(router_w, ((0, 0), (0, LANES - E)))
    return pl.pallas_call(
        _post_attn_kernel,
        grid=(T // tm,),
        in_specs=[
            pl.BlockSpec((tm, D), lambda i: (i, 0)),
            pl.BlockSpec((tm, D), lambda i: (i, 0)),
            pl.BlockSpec((1, 6, D), lambda i: (i // npb, 0, 0)),
            pl.BlockSpec((D, D), lambda i: (0, 0)),
            pl.BlockSpec((1, D), lambda i: (0, 0)),
            pl.BlockSpec((1, D), lambda i: (0, 0)),
            pl.BlockSpec((D, LANES), lambda i: (0, 0)),
        ],
        out_specs=[
            pl.BlockSpec((tm, D), lambda i: (i, 0)),
            pl.BlockSpec((tm, D), lambda i: (i, 0)),
            pl.BlockSpec((1, E, tm), lambda i: (i // npb, 0, i % npb)),
        ],
        out_shape=[
            jax.ShapeDtypeStruct((T, D), F32),
            jax.ShapeDtypeStruct((T, D), BF16),
            jax.ShapeDtypeStruct((B, E, S), F32),
        ],
        compiler_params=_cparams("parallel"),
        name="post_attn",
    )(o2, x2, mods3, w_o.astype(BF16), ln_g.reshape(1, D), ln_b.reshape(1, D), rw)


def _topk_kernel(aff_ref, idx_ref, gate_ref, *, cap):
    a3 = aff_ref[0]
    E, NC, _ = a3.shape
    R = E * NC

    def count3(mask3):
        c = jnp.sum(jnp.where(mask3, 1.0, 0.0), axis=2, keepdims=True)
        return jnp.sum(c, axis=1, keepdims=True)

    def body(i, bits):
        cand = bits | jnp.left_shift(jnp.int32(1), 30 - i)
        reach = count3(a3 >= lax.bitcast_convert_type(cand, F32))
        return jnp.where(reach >= cap, cand, bits)

    thr = lax.bitcast_convert_type(lax.fori_loop(0, 31, body, jnp.zeros((E, 1, 1), jnp.int32)), F32)
    gt3 = a3 > thr
    eq3 = a3 == thr
    need = cap - count3(gt3)

    li = lax.broadcasted_iota(jnp.int32, (LANES, LANES), 0)
    lj = lax.broadcasted_iota(jnp.int32, (LANES, LANES), 1)
    tri_excl = jnp.where(li < lj, 1.0, 0.0).astype(BF16)
    tri_incl = jnp.where(li <= lj, 1.0, 0.0).astype(BF16)
    ri = lax.broadcasted_iota(jnp.int32, (R, R), 0)
    rj = lax.broadcasted_iota(jnp.int32, (R, R), 1)
    same = (ri // NC) == (rj // NC)
    blk_lower = jnp.where(same & (rj < ri), 1.0, 0.0).astype(BF16)

    def chunk_offsets(mask2):
        tot = jnp.sum(mask2, axis=1, keepdims=True)
        totb = jnp.broadcast_to(tot, (R, LANES)).astype(BF16)
        return _dot(blk_lower, totb), tot

    eq2 = jnp.where(eq3, 1.0, 0.0).reshape(R, LANES)
    eq_off, _ = chunk_offsets(eq2)
    eq_rank = (eq_off + _dot(eq2.astype(BF16), tri_excl)).reshape(E, NC, LANES)
    sel3 = gt3 | (eq3 & (eq_rank < need))
    sel2 = jnp.where(sel3, 1.0, 0.0).reshape(R, LANES)
    off2, tot2 = chunk_offsets(sel2)
    cum2 = _dot(sel2.astype(BF16), tri_incl)
    a2 = a3.reshape(R, LANES)
    a_hi = a2.astype(BF16)
    a_mid = (a2 - a_hi.astype(F32)).astype(BF16)
    a_lo = (a2 - a_hi.astype(F32) - a_mid.astype(F32)).astype(BF16)

    cs = lax.broadcasted_iota(jnp.int32, (NC, cap), 1).astype(F32)
    ks = lax.broadcasted_iota(jnp.int32, (NC, cap), 0).astype(F32)
    ls = lax.broadcasted_iota(jnp.int32, (LANES, cap), 0).astype(F32)
    for e in range(E):
        rows = slice(e * NC, (e + 1) * NC)
        off_e = off2[rows, 0:1]
        end_e = off_e + tot2[rows]
        k_of_c = jnp.sum(jnp.where(end_e <= cs, 1.0, 0.0), axis=0, keepdims=True)
        onehot = ks == k_of_c
        oh = jnp.where(onehot, 1.0, 0.0).astype(BF16)
        off_c = jnp.sum(jnp.where(onehot, off_e, 0.0), axis=0, keepdims=True)
        rank_c = cs[0:1] - off_c
        cum_c = _dot(cum2[rows].T.astype(BF16), oh)
        lane_c = jnp.sum(jnp.where(cum_c <= rank_c, 1.0, 0.0), axis=0, keepdims=True)
        idx_ref[0, e] = (k_of_c * LANES + lane_c).astype(jnp.int32)
        g_c = (_dot(a_hi[rows].T, oh) + _dot(a_mid[rows].T, oh)) + _dot(a_lo[rows].T, oh)
        gate_ref[0, e] = jnp.sum(jnp.where(ls == lane_c, g_c, 0.0), axis=0, keepdims=True)


def _topk(aff, cap):
    B, E, S = aff.shape
    NC = S // LANES
    return pl.pallas_call(
        functools.partial(_topk_kernel, cap=cap),
        grid=(B,),
        in_specs=[pl.BlockSpec((1, E, NC, LANES), lambda b: (b, 0, 0, 0))],
        out_specs=[
            pl.BlockSpec((1, E, 1, cap), lambda b: (b, 0, 0, 0)),
            pl.BlockSpec((1, E, 1, cap), lambda b: (b, 0, 0, 0)),
        ],
        out_shape=[
            jax.ShapeDtypeStruct((B, E, 1, cap), jnp.int32),
            jax.ShapeDtypeStruct((B, E, 1, cap), F32),
        ],
        compiler_params=_cparams("parallel"),
        name="ec_topk",
    )(aff.reshape(B, E, NC, LANES))


def _dispatch_kernel(idx_ref, h_ref, xe_ref):
    idx = idx_ref[0, 0]
    cap = idx.shape[0]
    S = h_ref.shape[0]
    tok = lax.broadcasted_iota(jnp.int32, (cap, S), 1)
    onehot = jnp.where(tok == idx, 1.0, 0.0).astype(BF16)
    xe_ref[0] = _dot(onehot, h_ref[...]).astype(BF16)


def _dispatch(idx_col, h2, B, S, cap):
    E = idx_col.shape[1]
    D = h2.shape[1]
    return pl.pallas_call(
        _dispatch_kernel,
        grid=(B, E),
        in_specs=[
            pl.BlockSpec((1, 1, cap, 1), lambda b, e: (b, e, 0, 0)),
            pl.BlockSpec((S, D), lambda b, e: (b, 0)),
        ],
        out_specs=pl.BlockSpec((1, cap, D), lambda b, e: (e, b, 0)),
        out_shape=jax.ShapeDtypeStruct((E, B * cap, D), BF16),
        compiler_params=_cparams("parallel", "parallel"),
        name="moe_dispatch",
    )(idx_col, h2)


def _moe_ffn_kernel(xe_ref, gate_ref, wg_ref, wu_ref, wo_ref, y_ref, acc_ref):
    f = pl.program_id(1)

    @pl.when(f == 0)
    def _():
        acc_ref[...] = jnp.zeros_like(acc_ref)

    xe = xe_ref[0]
    g = _dot(xe, wg_ref[0, 0].astype(BF16))
    u = _dot(xe, wu_ref[0, 0].astype(BF16))
    act = (g * (1.0 / (1.0 + jnp.exp(-g))) * u).astype(BF16)
    acc_ref[...] += _dot(act, wo_ref[0, 0].astype(BF16))

    @pl.when(f == pl.num_programs(1) - 1)
    def _():
        y_ref[0] = (acc_ref[...] * gate_ref[0]).astype(BF16)


def _moe_ffn(xe, gate_col, w_in, w_out, layer):
    E, M, D = xe.shape
    FF = w_out.shape[2]
    tf = FF_TILE
    nf = FF // tf
    return pl.pallas_call(
        _moe_ffn_kernel,
        grid=(E, nf),
        in_specs=[
            pl.BlockSpec((1, M, D), lambda e, f: (e, 0, 0)),
            pl.BlockSpec((1, M, 1), lambda e, f: (e, 0, 0)),
            pl.BlockSpec((1, 1, D, tf), lambda e, f: (layer, e, 0, f)),
            pl.BlockSpec((1, 1, D, tf), lambda e, f: (layer, e, 0, nf + f)),
            pl.BlockSpec((1, 1, tf, D), lambda e, f: (layer, e, f, 0)),
        ],
        out_specs=pl.BlockSpec((1, M, D), lambda e, f: (e, 0, 0)),
        out_shape=jax.ShapeDtypeStruct((E, M, D), BF16),
        scratch_shapes=[pltpu.VMEM((M, D), F32)],
        compiler_params=_cparams("parallel", "arbitrary"),
        name="moe_ffn",
    )(xe, gate_col, w_in, w_in, w_out)


def _combine_kernel(idx_ref, y_ref, f_ref, *, tile):
    e = pl.program_id(1)

    @pl.when(e == 0)
    def _():
        f_ref[...] = jnp.zeros_like(f_ref)

    idx = idx_ref[0, 0]
    y = y_ref[0]
    cap = y.shape[0]
    S = f_ref.shape[0]
    tok = lax.broadcasted_iota(jnp.int32, (tile, cap), 0)
    for j in range(S // tile):
        onehot = jnp.where(tok + j * tile == idx, 1.0, 0.0).astype(BF16)
        f_ref[j * tile:(j + 1) * tile, :] += _dot(onehot, y)


def _combine(idx, y, B, S, cap):
    E, _, D = y.shape
    tile = min(COMBINE_TILE, S)
    return pl.pallas_call(
        functools.partial(_combine_kernel, tile=tile),
        grid=(B, E),
        in_specs=[
            pl.BlockSpec((1, 1, 1, cap), lambda b, e: (b, e, 0, 0)),
            pl.BlockSpec((1, cap, D), lambda b, e: (e, b, 0)),
        ],
        out_specs=pl.BlockSpec((S, D), lambda b, e: (b, 0)),
        out_shape=jax.ShapeDtypeStruct((B * S, D), F32),
        compiler_params=_cparams("parallel", "arbitrary"),
        name="moe_combine",
    )(idx, y)


def _post_moe_kernel(x_ref, f_ref, mod_ref, g_ref, b_ref, o_ref):
    m = mod_ref[0]
    o_ref[...] = _layer_norm(ALPHA * x_ref[...] + (1.0 + m[5:6]) * f_ref[...], g_ref[...], b_ref[...])


def _post_moe(x1, f, mods3, ln_g, ln_b, S):
    T, D = x1.shape
    tm = min(TOKEN_TILE, S)
    npb = S // tm
    row = pl.BlockSpec((tm, D), lambda i: (i, 0))
    vec = pl.BlockSpec((1, D), lambda i: (0, 0))
    return pl.pallas_call(
        _post_moe_kernel,
        grid=(T // tm,),
        in_specs=[row, row, pl.BlockSpec((1, 6, D), lambda i: (i // npb, 0, 0)), vec, vec],
        out_specs=row,
        out_shape=jax.ShapeDtypeStruct((T, D), F32),
        compiler_params=_cparams("parallel"),
        name="post_moe",
    )(x1, f, mods3, ln_g.reshape(1, D), ln_b.reshape(1, D))


def _rope_tables(positions):
    B, S = positions.shape
    pos = positions.astype(F32)[..., None]
    inv_m = MLA_THETA ** (-jnp.arange(0, MLA_ROPE, 2, dtype=F32) / MLA_ROPE)
    ang_m = (pos * inv_m).reshape(B * S, MLA_ROPE // 2)
    cos_m = jnp.tile(jnp.cos(ang_m), (1, LANES // (MLA_ROPE // 2)))
    sin_m = jnp.tile(jnp.sin(ang_m), (1, LANES // (MLA_ROPE // 2)))
    inv_d = ROPE_THETA ** (-jnp.arange(0, DIFF_ROT, 2, dtype=F32) / DIFF_ROT)
    ang_d = (pos * inv_d).reshape(B * S, DIFF_ROT // 2)
    cd, sd = jnp.cos(ang_d), jnp.sin(ang_d)
    hr = DIFF_ROT // 2
    rest = DIFF_HEAD_DIM - DIFF_ROT
    ones = jnp.ones((B * S, rest), F32)
    zeros = jnp.zeros((B * S, rest), F32)
    zr = jnp.zeros((B * S, hr), F32)
    rep = LANES // DIFF_HEAD_DIM
    c_d = jnp.tile(jnp.concatenate([cd, cd, ones], axis=1), (1, rep))
    sa_d = jnp.tile(jnp.concatenate([-sd, zr, zeros], axis=1), (1, rep))
    sb_d = jnp.tile(jnp.concatenate([zr, sd, zeros], axis=1), (1, rep))
    return cos_m, sin_m, c_d, sa_d, sb_d


def _moe_block(x1, h2, aff, mods3, w_in, w_out, layer, ln_g, ln_b, B, S):
    E = N_EXPERTS
    cap = max(1, EC_CAPACITY_FACTOR * S // E)
    idx, gate = _topk(aff, cap)
    gate_col = gate[:, :, 0, :].transpose(1, 0, 2).reshape(E, B * cap, 1)
    xe = _dispatch(idx.reshape(B, E, cap, 1), h2, B, S, cap)
    y = _moe_ffn(xe, gate_col, w_in, w_out, layer)
    f = _combine(idx, y, B, S, cap)
    return _post_moe(x1, f, mods3, ln_g, ln_b, S)


def kernel(x, c, positions, ada_w, ada_b, ln1_g, ln1_b, ln2_g, ln2_b, mla_w_in, mla_q_norm, mla_kv_norm,
           mla_w_uq, mla_w_ukv, mla_w_o, diff_w_in, diff_lambda, diff_subln, diff_w_o, router_w, moe_w_in,
           moe_w_out):
    B, S, D = x.shape
    cos_m, sin_m, c_d, sa_d, sb_d = _rope_tables(positions)
    mods = _mods(c, ada_w, ada_b)
    x2 = x.reshape(B * S, D)
    for i in range(DEPTH):
        mods3 = mods[i, :B].reshape(B, 6, D)
        j = i // 2
        if i % 2 == 0:
            q, k, v = _mla_proj(x2, mods3, cos_m, sin_m, mla_w_in[j], mla_q_norm[j], mla_kv_norm[j],
                                mla_w_uq[j], mla_w_ukv[j], B, S)
            o = _attention(_mla_attn_kernel, [], q, k, v, "mla_attn")
            w_o = mla_w_o[j]
        else:
            lambda_init = 0.8 - 0.6 * math.exp(-0.3 * i)
            q, k, v = _diff_proj(x2, mods3, c_d, sa_d, sb_d, diff_w_in[j], B, S)
            o = _attention(functools.partial(_diff_attn_kernel, lambda_init=lambda_init),
                           [diff_lambda[j], diff_subln[j].reshape(1, DIFF_V)], q, k, v, "diff_attn")
            w_o = diff_w_o[j]
        x1, h2, aff = _post_attn(o.reshape(B * S, D), x2, mods3, w_o, ln1_g[i], ln1_b[i], router_w[i], B, S)
        x2 = _moe_block(x1, h2, aff, mods3, moe_w_in, moe_w_out, i, ln2_g[i], ln2_b[i], B, S)
    return x2.reshape(B, S, D)
```

```python
import functools
import math

import jax
import jax.numpy as jnp
from jax import lax
from jax.experimental import pallas as pl
from jax.experimental.pallas import tpu as pltpu

F32 = jnp.float32
BF16 = jnp.bfloat16

D_MODEL = 1024
DEPTH = 2
MLA_HEADS = 8
MLA_Q_LORA = 384
MLA_KV_LORA = 256
MLA_NOPE = 128
MLA_ROPE = 64
MLA_V = 128
MLA_THETA = 10000.0
DIFF_HEADS = 8
DIFF_HEAD_DIM = 64
DIFF_V = 2 * DIFF_HEAD_DIM
DIFF_ROT = DIFF_HEAD_DIM // 4
ROPE_THETA = 500000.0
N_EXPERTS = 16
EXPERT_FF = 2048
EC_CAPACITY_FACTOR = 2
NORM_EPS = 1e-5
LATENT_EPS = 1e-6
ALPHA = (2 * DEPTH) ** 0.25

LANES = 128
VMEM_LIMIT = 56 * 1024 * 1024

TOKEN_TILE = 512
MLA_Q_TILE = 1024
DIFF_Q_TILE = 512
KEY_CHUNK = 512
ROW_BLOCK = 256
LOG2E = math.log2(math.e)
FF_TILE = 512
SLOT_ALIGN = 16
COMBINE_WINDOW = 256


def _cparams(*sem):
    return pltpu.CompilerParams(dimension_semantics=sem, vmem_limit_bytes=VMEM_LIMIT)


def _dot(a, b):
    return jnp.dot(a, b, preferred_element_type=F32)


def _split_bf16(a):
    hi = a.astype(BF16)
    lo = (a - hi.astype(F32)).astype(BF16)
    return hi, lo


def _dot3(a, b):
    ah, al = _split_bf16(a)
    bh, bl = _split_bf16(b)
    return _dot(ah, bh) + (_dot(ah, bl) + _dot(al, bh))


def _layer_norm(y, g, b):
    mu = jnp.mean(y, axis=-1, keepdims=True)
    yc = y - mu
    var = jnp.mean(yc * yc, axis=-1, keepdims=True)
    return yc * lax.rsqrt(var + NORM_EPS) * g + b


def _mods_kernel(c_ref, w_ref, b_ref, o_ref):
    c = c_ref[...]
    a = c * (1.0 / (1.0 + jnp.exp(-c)))
    o_ref[0] = _dot3(a, w_ref[0]) + b_ref[0]


def _mods(c, ada_w, ada_b):
    B, D = c.shape
    L, _, N = ada_w.shape
    rows = 8
    cp = jnp.zeros((rows, D), F32).at[:B].set(c)
    tn = 1536
    return pl.pallas_call(
        _mods_kernel,
        grid=(L, N // tn),
        in_specs=[
            pl.BlockSpec((rows, D), lambda l, j: (0, 0)),
            pl.BlockSpec((1, D, tn), lambda l, j: (l, 0, j)),
            pl.BlockSpec((1, 1, tn), lambda l, j: (l, 0, j)),
        ],
        out_specs=pl.BlockSpec((1, rows, tn), lambda l, j: (l, 0, j)),
        out_shape=jax.ShapeDtypeStruct((L, rows, N), F32),
        compiler_params=_cparams("parallel", "parallel"),
        name="adaln_mods",
    )(cp, ada_w, ada_b.reshape(L, 1, N))


def _mla_proj_kernel(x_ref, mod_ref, cos_ref, sin_ref, win_ref, qn_ref, kvn_ref, wq_ref, wkv_ref,
                     q_ref, k_ref, v_ref, *, scale):
    m = mod_ref[0]
    h = (x_ref[...] * (1.0 + m[1:2]) + m[0:1]).astype(BF16)
    z = _dot(h, win_ref[...])
    cq = z[:, :MLA_Q_LORA]
    ckv = z[:, MLA_Q_LORA:MLA_Q_LORA + MLA_KV_LORA]
    kr = z[:, 640:768]
    krs = z[:, 768:896]
    cos = cos_ref[...]
    sin = sin_ref[...]
    qn = (cq * lax.rsqrt(jnp.mean(cq * cq, axis=-1, keepdims=True) + LATENT_EPS) * qn_ref[...]).astype(BF16)
    kvn = (ckv * lax.rsqrt(jnp.mean(ckv * ckv, axis=-1, keepdims=True) + LATENT_EPS) * kvn_ref[...]).astype(BF16)
    q = _dot(qn, wq_ref[...])
    kv = _dot(kvn, wkv_ref[...])
    kpe = (kr * cos + krs * sin).astype(BF16)
    hw = MLA_HEADS * LANES
    for hh in range(MLA_HEADS):
        a, b = hh * LANES, (hh + 1) * LANES
        q_ref[0, hh, :, 0:LANES] = (q[:, a:b] * scale).astype(BF16)
        q_ref[0, hh, :, LANES:2 * LANES] = (
            (q[:, hw + a:hw + b] * cos + q[:, 2 * hw + a:2 * hw + b] * sin) * scale).astype(BF16)
        k_ref[0, hh, :, 0:LANES] = kv[:, a:b].astype(BF16)
        k_ref[0, hh, :, LANES:2 * LANES] = kpe
        v_ref[0, hh] = kv[:, hw + a:hw + b].astype(BF16)


def _mla_proj(x2, mods3, cos_t, sin_t, w_in, q_norm, kv_norm, w_uq, w_ukv, B, S):
    T, D = x2.shape
    H = MLA_HEADS
    tm = min(TOKEN_TILE, S)
    npb = S // tm
    half = MLA_ROPE // 2
    kr = w_in[:, 640:704]
    kr_sw = jnp.concatenate([-kr[:, half:], kr[:, :half]], axis=1)
    z64 = jnp.zeros((D, 64), F32)
    win_p = jnp.concatenate([w_in[:, :640], kr, z64, kr_sw, z64], axis=1).astype(BF16)
    wq3 = w_uq.reshape(MLA_Q_LORA, H, MLA_NOPE + MLA_ROPE)
    rope = wq3[:, :, MLA_NOPE:]
    rope_sw = jnp.concatenate([-rope[:, :, half:], rope[:, :, :half]], axis=2)
    pad = lambda t: jnp.pad(t, ((0, 0), (0, 0), (0, LANES - MLA_ROPE))).reshape(MLA_Q_LORA, H * LANES)
    wq_p = jnp.concatenate([wq3[:, :, :MLA_NOPE].reshape(MLA_Q_LORA, H * MLA_NOPE), pad(rope), pad(rope_sw)],
                           axis=1).astype(BF16)
    wkv3 = w_ukv.reshape(MLA_KV_LORA, H, MLA_NOPE + MLA_V)
    wkv_p = jnp.concatenate([wkv3[:, :, :MLA_NOPE].reshape(MLA_KV_LORA, H * MLA_NOPE),
                             wkv3[:, :, MLA_NOPE:].reshape(MLA_KV_LORA, H * MLA_V)], axis=1).astype(BF16)
    scale = (MLA_NOPE + MLA_ROPE) ** -0.5 * LOG2E
    full = lambda shape: pl.BlockSpec(shape, lambda i: (0,) * len(shape))
    return pl.pallas_call(
        functools.partial(_mla_proj_kernel, scale=scale),
        grid=(T // tm,),
        in_specs=[
            pl.BlockSpec((tm, D), lambda i: (i, 0)),
            pl.BlockSpec((1, 6, D), lambda i: (i // npb, 0, 0)),
            pl.BlockSpec((tm, LANES), lambda i: (i, 0)),
            pl.BlockSpec((tm, LANES), lambda i: (i, 0)),
            full(win_p.shape),
            full((1, MLA_Q_LORA)),
            full((1, MLA_KV_LORA)),
            full(wq_p.shape),
            full(wkv_p.shape),
        ],
        out_specs=[
            pl.BlockSpec((1, H, tm, 2 * LANES), lambda i: (i // npb, 0, i % npb, 0)),
            pl.BlockSpec((1, H, tm, 2 * LANES), lambda i: (i // npb, 0, i % npb, 0)),
            pl.BlockSpec((1, H, tm, MLA_V), lambda i: (i // npb, 0, i % npb, 0)),
        ],
        out_shape=[
            jax.ShapeDtypeStruct((B, H, S, 2 * LANES), BF16),
            jax.ShapeDtypeStruct((B, H, S, 2 * LANES), BF16),
            jax.ShapeDtypeStruct((B, H, S, MLA_V), BF16),
        ],
        compiler_params=_cparams("parallel"),
        name="mla_proj",
    )(x2, mods3, cos_t, sin_t, win_p, q_norm.reshape(1, -1), kv_norm.reshape(1, -1), wq_p, wkv_p)


_NT = (((1,), (1,)), ((), ()))


def _softmax_pv(q, k_ref, v_ref):
    S = k_ref.shape[2]
    tk = min(KEY_CHUNK, S)
    chunks = [slice(j * tk, (j + 1) * tk) for j in range(S // tk)]
    lane_blocks = [slice(c * LANES, (c + 1) * LANES) for c in range(tk // LANES)]
    accs, sums = [], []
    for r0 in range(0, q.shape[0], ROW_BLOCK):
        s = lax.dot_general(q[r0:r0 + ROW_BLOCK], k_ref[0, 0], _NT, preferred_element_type=F32)
        m = jnp.max(s, axis=-1, keepdims=True)
        ls = []
        acc = None
        for ck in chunks:
            e = jnp.exp2(s[:, ck] - m)
            lj = e[:, lane_blocks[0]]
            for lb in lane_blocks[1:]:
                lj = lj + e[:, lb]
            ls.append(lj)
            pv = _dot(e.astype(BF16), v_ref[0, 0, ck, :])
            acc = pv if acc is None else acc + pv
        while len(ls) > 1:
            ls = [a + b for a, b in zip(ls[0::2], ls[1::2])]
        accs.append(acc)
        sums.append(jnp.sum(ls[0], axis=-1, keepdims=True))
    return jnp.concatenate(accs, axis=0), jnp.concatenate(sums, axis=0)


def _mla_attn_kernel(q_ref, k_ref, v_ref, o_ref):
    acc, l = _softmax_pv(q_ref[0, 0], k_ref, v_ref)
    o_ref[0] = (acc * (1.0 / l)).astype(BF16)


def _diff_attn_kernel(lam_ref, subln_ref, q_ref, k_ref, v_ref, o_ref, *, lambda_init):
    q = q_ref[0, 0]
    tq = q.shape[0]
    lane = lax.broadcasted_iota(jnp.int32, q.shape, 1)
    zero = jnp.zeros_like(q)
    qs = jnp.concatenate([jnp.where(lane < DIFF_HEAD_DIM, q, zero),
                          jnp.where(lane >= DIFF_HEAD_DIM, q, zero)], axis=0)
    acc, l = _softmax_pv(qs, k_ref, v_ref)
    r = 1.0 / l
    lf = lam_ref[...]
    lam = (jnp.exp(jnp.sum(lf[0:1] * lf[1:2], axis=-1, keepdims=True))
           - jnp.exp(jnp.sum(lf[2:3] * lf[3:4], axis=-1, keepdims=True)) + lambda_init)
    o = acc[:tq] * r[:tq] - acc[tq:] * (lam * r[tq:])
    o = o * lax.rsqrt(jnp.mean(o * o, axis=-1, keepdims=True) + NORM_EPS) * subln_ref[...]
    o_ref[0] = (o * (1.0 - lambda_init)).astype(BF16)


def _attention(kernel_fn, extra, q, k, v, q_tile, name):
    B, H, S, dq = q.shape
    dv = v.shape[-1]
    tq = min(q_tile, S)
    extra_specs = [pl.BlockSpec(a.shape, lambda b, h, i, n=a.ndim: (0,) * n) for a in extra]
    return pl.pallas_call(
        kernel_fn,
        grid=(B, H, S // tq),
        in_specs=extra_specs + [
            pl.BlockSpec((1, 1, tq, dq), lambda b, h, i: (b, h, i, 0)),
            pl.BlockSpec((1, 1, S, dq), lambda b, h, i: (b, h, 0, 0)),
            pl.BlockSpec((1, 1, S, dv), lambda b, h, i: (b, h, 0, 0)),
        ],
        out_specs=pl.BlockSpec((1, tq, dv), lambda b, h, i: (b, i, h)),
        out_shape=jax.ShapeDtypeStruct((B, S, H * dv), BF16),
        compiler_params=_cparams("parallel", "parallel", "parallel"),
        name=name,
    )(*extra, q, k, v)


def _diff_proj_kernel(x_ref, mod_ref, c_ref, sa_ref, sb_ref, win_ref, q_ref, k_ref, v_ref, *, scale):
    m = mod_ref[0]
    h = (x_ref[...] * (1.0 + m[1:2]) + m[0:1]).astype(BF16)
    z = _dot(h, win_ref[...])
    c = c_ref[...]
    sa = sa_ref[...]
    sb = sb_ref[...]
    hw = DIFF_HEADS * LANES

    def rope(t):
        return t * c + pltpu.roll(t, LANES - DIFF_ROT // 2, 1) * sa + pltpu.roll(t, DIFF_ROT // 2, 1) * sb

    for hh in range(DIFF_HEADS):
        a, b = hh * LANES, (hh + 1) * LANES
        q_ref[0, hh] = (rope(z[:, a:b]) * scale).astype(BF16)
        k_ref[0, hh] = rope(z[:, hw + a:hw + b]).astype(BF16)
        v_ref[0, hh] = z[:, 2 * hw + a:2 * hw + b].astype(BF16)


def _diff_proj(x2, mods3, c_t, sa_t, sb_t, w_in, B, S):
    T, D = x2.shape
    H = DIFF_HEADS
    tm = min(TOKEN_TILE, S)
    npb = S // tm
    win = w_in.astype(BF16)
    hd = jax.ShapeDtypeStruct((B, H, S, LANES), BF16)
    hspec = pl.BlockSpec((1, H, tm, LANES), lambda i: (i // npb, 0, i % npb, 0))
    return pl.pallas_call(
        functools.partial(_diff_proj_kernel, scale=DIFF_HEAD_DIM ** -0.5 * LOG2E),
        grid=(T // tm,),
        in_specs=[
            pl.BlockSpec((tm, D), lambda i: (i, 0)),
            pl.BlockSpec((1, 6, D), lambda i: (i // npb, 0, 0)),
            pl.BlockSpec((tm, LANES), lambda i: (i, 0)),
            pl.BlockSpec((tm, LANES), lambda i: (i, 0)),
            pl.BlockSpec((tm, LANES), lambda i: (i, 0)),
            pl.BlockSpec(win.shape, lambda i: (0, 0)),
        ],
        out_specs=[hspec, hspec, hspec],
        out_shape=[hd, hd, hd],
        compiler_params=_cparams("parallel"),
        name="diff_proj",
    )(x2, mods3, c_t, sa_t, sb_t, win)


def _post_attn_kernel(o_ref, x_ref, mod_ref, wo_ref, g_ref, b_ref, rw_ref, x1_ref, h2_ref, aff_ref):
    m = mod_ref[0]
    t = _dot(o_ref[...], wo_ref[...])
    x1 = _layer_norm(ALPHA * x_ref[...] + (1.0 + m[2:3]) * t, g_ref[...], b_ref[...])
    x1_ref[...] = x1
    h2 = x1 * (1.0 + m[4:5]) + m[3:4]
    h2_ref[...] = h2.astype(BF16)
    logits = _dot3(h2, rw_ref[...])
    lt = logits.T[:N_EXPERTS]
    mx = jnp.max(lt, axis=0, keepdims=True)
    ex = jnp.exp(lt - mx)
    aff_ref[0] = ex / jnp.sum(ex, axis=0, keepdims=True)


def _post_attn(o2, x2, mods3, w_o, ln_g, ln_b, router_w, B, S):
    T, D = x2.shape
    E = N_EXPERTS
    tm = min(TOKEN_TILE, S)
    npb = S // tm
    rw = jnp.pad(router_w, ((0, 0), (0, LANES - E)))
    return pl.pallas_call(
        _post_attn_kernel,
        grid=(T // tm,),
        in_specs=[
            pl.BlockSpec((tm, D), lambda i: (i, 0)),
            pl.BlockSpec((tm, D), lambda i: (i, 0)),
            pl.BlockSpec((1, 6, D), lambda i: (i // npb, 0, 0)),
            pl.BlockSpec((D, D), lambda i: (0, 0)),
            pl.BlockSpec((1, D), lambda i: (0, 0)),
            pl.BlockSpec((1, D), lambda i: (0, 0)),
            pl.BlockSpec((D, LANES), lambda i: (0, 0)),
        ],
        out_specs=[
            pl.BlockSpec((tm, D), lambda i: (i, 0)),
            pl.BlockSpec((tm, D), lambda i: (i, 0)),
            pl.BlockSpec((1, E, tm), lambda i: (i // npb, 0, i % npb)),
        ],
        out_shape=[
            jax.ShapeDtypeStruct((T, D), F32),
            jax.ShapeDtypeStruct((T, D), BF16),
            jax.ShapeDtypeStruct((B, E, S), F32),
        ],
        compiler_params=_cparams("parallel"),
        name="post_attn",
    )(o2, x2, mods3, w_o.astype(BF16), ln_g.reshape(1, D), ln_b.reshape(1, D), rw)


def _topk_kernel(aff_ref, idx_ref, gate_ref, pos_ref, off_ref, *, cap):
    a3 = aff_ref[0]
    E, NC, _ = a3.shape
    R = E * NC

    def count3(mask3):
        c = jnp.sum(jnp.where(mask3, 1.0, 0.0), axis=2, keepdims=True)
        return jnp.sum(c, axis=1, keepdims=True)

    def body(i, bits):
        cand = bits | jnp.left_shift(jnp.int32(1), 30 - i)
        reach = count3(a3 >= lax.bitcast_convert_type(cand, F32))
        return jnp.where(reach >= cap, cand, bits)

    thr = lax.bitcast_convert_type(lax.fori_loop(0, 31, body, jnp.zeros((E, 1, 1), jnp.int32)), F32)
    gt3 = a3 > thr
    eq3 = a3 == thr
    need = cap - count3(gt3)

    li = lax.broadcasted_iota(jnp.int32, (LANES, LANES), 0)
    lj = lax.broadcasted_iota(jnp.int32, (LANES, LANES), 1)
    tri_excl = jnp.where(li < lj, 1.0, 0.0).astype(BF16)
    tri_incl = jnp.where(li <= lj, 1.0, 0.0).astype(BF16)
    ri = lax.broadcasted_iota(jnp.int32, (R, R), 0)
    rj = lax.broadcasted_iota(jnp.int32, (R, R), 1)
    same = (ri // NC) == (rj // NC)
    blk_lower = jnp.where(same & (rj < ri), 1.0, 0.0).astype(BF16)

    def chunk_offsets(mask2):
        tot = jnp.sum(mask2, axis=1, keepdims=True)
        totb = jnp.broadcast_to(tot, (R, LANES)).astype(BF16)
        return _dot(blk_lower, totb), tot

    eq2 = jnp.where(eq3, 1.0, 0.0).reshape(R, LANES)
    eq_off, _ = chunk_offsets(eq2)
    eq_rank = (eq_off + _dot(eq2.astype(BF16), tri_excl)).reshape(E, NC, LANES)
    sel3 = gt3 | (eq3 & (eq_rank < need))
    sel2 = jnp.where(sel3, 1.0, 0.0).reshape(R, LANES)
    off2, tot2 = chunk_offsets(sel2)
    cum2 = _dot(sel2.astype(BF16), tri_incl)
    pos_ref[0] = jnp.where(sel3, (off2 + cum2 - 1.0).reshape(E, NC, LANES), -1.0).astype(jnp.int32)
    off_ref[0] = off2.reshape(E, NC, LANES).astype(jnp.int32)
    a2 = a3.reshape(R, LANES)
    a_hi = a2.astype(BF16)
    a_mid = (a2 - a_hi.astype(F32)).astype(BF16)
    a_lo = (a2 - a_hi.astype(F32) - a_mid.astype(F32)).astype(BF16)

    cs = lax.broadcasted_iota(jnp.int32, (NC, cap), 1).astype(F32)
    ks = lax.broadcasted_iota(jnp.int32, (NC, cap), 0).astype(F32)
    ls = lax.broadcasted_iota(jnp.int32, (LANES, cap), 0).astype(F32)
    for e in range(E):
        rows = slice(e * NC, (e + 1) * NC)
        off_e = off2[rows, 0:1]
        end_e = off_e + tot2[rows]
        k_of_c = jnp.sum(jnp.where(end_e <= cs, 1.0, 0.0), axis=0, keepdims=True)
        onehot = ks == k_of_c
        oh = jnp.where(onehot, 1.0, 0.0).astype(BF16)
        off_c = jnp.sum(jnp.where(onehot, off_e, 0.0), axis=0, keepdims=True)
        rank_c = cs[0:1] - off_c
        cum_c = _dot(cum2[rows].T.astype(BF16), oh)
        lane_c = jnp.sum(jnp.where(cum_c <= rank_c, 1.0, 0.0), axis=0, keepdims=True)
        idx_ref[0, e] = (k_of_c * LANES + lane_c).astype(jnp.int32)
        g_c = (_dot(a_hi[rows].T, oh) + _dot(a_mid[rows].T, oh)) + _dot(a_lo[rows].T, oh)
        gate_ref[0, e] = jnp.sum(jnp.where(ls == lane_c, g_c, 0.0), axis=0, keepdims=True)


def _topk(aff, cap):
    B, E, S = aff.shape
    NC = S // LANES
    slot_spec = pl.BlockSpec((1, E, 1, cap), lambda b: (b, 0, 0, 0))
    tok_spec = pl.BlockSpec((1, E, NC, LANES), lambda b: (b, 0, 0, 0))
    return pl.pallas_call(
        functools.partial(_topk_kernel, cap=cap),
        grid=(B,),
        in_specs=[tok_spec],
        out_specs=[slot_spec, slot_spec, tok_spec, tok_spec],
        out_shape=[
            jax.ShapeDtypeStruct((B, E, 1, cap), jnp.int32),
            jax.ShapeDtypeStruct((B, E, 1, cap), F32),
            jax.ShapeDtypeStruct((B, E, NC, LANES), jnp.int32),
            jax.ShapeDtypeStruct((B, E, NC, LANES), jnp.int32),
        ],
        compiler_params=_cparams("parallel"),
        name="ec_topk",
    )(aff.reshape(B, E, NC, LANES))


def _dispatch_kernel(idx_ref, h_ref, xe_ref):
    idx = idx_ref[0, 0]
    cap = idx.shape[0]
    S = h_ref.shape[0]
    tok = lax.broadcasted_iota(jnp.int32, (cap, S), 1)
    onehot = jnp.where(tok == idx, 1.0, 0.0).astype(BF16)
    xe_ref[0] = _dot(onehot, h_ref[...]).astype(BF16)


def _dispatch(idx_col, h2, B, S, cap):
    E = idx_col.shape[1]
    D = h2.shape[1]
    return pl.pallas_call(
        _dispatch_kernel,
        grid=(B, E),
        in_specs=[
            pl.BlockSpec((1, 1, cap, 1), lambda b, e: (b, e, 0, 0)),
            pl.BlockSpec((S, D), lambda b, e: (b, 0)),
        ],
        out_specs=pl.BlockSpec((1, cap, D), lambda b, e: (e, b, 0)),
        out_shape=jax.ShapeDtypeStruct((E, B * cap, D), BF16),
        compiler_params=_cparams("parallel", "parallel"),
        name="moe_dispatch",
    )(idx_col, h2)


def _moe_ffn_kernel(xe_ref, gate_ref, wg_ref, wu_ref, wo_ref, y_ref, acc_ref):
    f = pl.program_id(1)

    @pl.when(f == 0)
    def _():
        acc_ref[...] = jnp.zeros_like(acc_ref)

    xe = xe_ref[0]
    g = _dot(xe, wg_ref[0, 0].astype(BF16))
    u = _dot(xe, wu_ref[0, 0].astype(BF16))
    act = (g * (1.0 / (1.0 + jnp.exp(-g))) * u).astype(BF16)
    acc_ref[...] += _dot(act, wo_ref[0, 0].astype(BF16))

    @pl.when(f == pl.num_programs(1) - 1)
    def _():
        y_ref[0] = (acc_ref[...] * gate_ref[0]).astype(BF16)


def _moe_ffn(xe, gate_col, w_in, w_out, layer):
    E, M, D = xe.shape
    FF = w_out.shape[2]
    tf = FF_TILE
    nf = FF // tf
    return pl.pallas_call(
        _moe_ffn_kernel,
        grid=(E, nf),
        in_specs=[
            pl.BlockSpec((1, M, D), lambda e, f: (e, 0, 0)),
            pl.BlockSpec((1, M, 1), lambda e, f: (e, 0, 0)),
            pl.BlockSpec((1, 1, D, tf), lambda e, f: (layer, e, 0, f)),
            pl.BlockSpec((1, 1, D, tf), lambda e, f: (layer, e, 0, nf + f)),
            pl.BlockSpec((1, 1, tf, D), lambda e, f: (layer, e, f, 0)),
        ],
        out_specs=pl.BlockSpec((1, M, D), lambda e, f: (e, 0, 0)),
        out_shape=jax.ShapeDtypeStruct((E, M, D), BF16),
        scratch_shapes=[pltpu.VMEM((M, D), F32)],
        compiler_params=_cparams("parallel", "arbitrary"),
        name="moe_ffn",
    )(xe, gate_col, w_in, w_in, w_out)


def _combine_ln_kernel(win_ref, pos_ref, y_ref, x_ref, mod_ref, g_ref, b_ref, o_ref, *, window):
    b = pl.program_id(0)
    j = pl.program_id(1)
    n_exp = y_ref.shape[0]
    pos = pos_ref[0]
    slot = lax.broadcasted_iota(jnp.int32, (pos.shape[0], window), 1)
    f = None
    for e in range(n_exp):
        start = pl.multiple_of(win_ref[(b * pl.num_programs(1) + j) * n_exp + e], SLOT_ALIGN)
        onehot = jnp.where(pos[:, e:e + 1] - start == slot, 1.0, 0.0).astype(BF16)
        part = _dot(onehot, y_ref[e, pl.ds(start, window), :])
        f = part if f is None else f + part
    m = mod_ref[0]
    o_ref[...] = _layer_norm(ALPHA * x_ref[...] + (1.0 + m[5:6]) * f, g_ref[...], b_ref[...])


def _combine_ln(pos_t, win, y, x1, mods3, ln_g, ln_b, B, S, cap, window):
    E, _, D = y.shape
    nj = S // LANES
    row = pl.BlockSpec((LANES, D), lambda b, j, w: (b * nj + j, 0))
    vec = pl.BlockSpec((1, D), lambda b, j, w: (0, 0))
    return pl.pallas_call(
        functools.partial(_combine_ln_kernel, window=window),
        grid_spec=pltpu.PrefetchScalarGridSpec(
            num_scalar_prefetch=1,
            grid=(B, nj),
            in_specs=[
                pl.BlockSpec((1, LANES, E), lambda b, j, w: (b, j, 0)),
                pl.BlockSpec((E, cap, D), lambda b, j, w: (0, b, 0)),
                row,
                pl.BlockSpec((1, 6, D), lambda b, j, w: (b, 0, 0)),
                vec,
                vec,
            ],
            out_specs=row,
        ),
        out_shape=jax.ShapeDtypeStruct((B * S, D), F32),
        compiler_params=_cparams("parallel", "arbitrary"),
        name="moe_combine_ln",
    )(win, pos_t, y, x1, mods3, ln_g.reshape(1, D), ln_b.reshape(1, D))


def _rope_tables(positions):
    B, S = positions.shape
    pos = positions.astype(F32)[..., None]
    inv_m = MLA_THETA ** (-jnp.arange(0, MLA_ROPE, 2, dtype=F32) / MLA_ROPE)
    ang_m = (pos * inv_m).reshape(B * S, MLA_ROPE // 2)
    cos_m = jnp.tile(jnp.cos(ang_m), (1, LANES // (MLA_ROPE // 2)))
    sin_m = jnp.tile(jnp.sin(ang_m), (1, LANES // (MLA_ROPE // 2)))
    inv_d = ROPE_THETA ** (-jnp.arange(0, DIFF_ROT, 2, dtype=F32) / DIFF_ROT)
    ang_d = (pos * inv_d).reshape(B * S, DIFF_ROT // 2)
    cd, sd = jnp.cos(ang_d), jnp.sin(ang_d)
    hr = DIFF_ROT // 2
    rest = DIFF_HEAD_DIM - DIFF_ROT
    ones = jnp.ones((B * S, rest), F32)
    zeros = jnp.zeros((B * S, rest), F32)
    zr = jnp.zeros((B * S, hr), F32)
    rep = LANES // DIFF_HEAD_DIM
    c_d = jnp.tile(jnp.concatenate([cd, cd, ones], axis=1), (1, rep))
    sa_d = jnp.tile(jnp.concatenate([-sd, zr, zeros], axis=1), (1, rep))
    sb_d = jnp.tile(jnp.concatenate([zr, sd, zeros], axis=1), (1, rep))
    return cos_m, sin_m, c_d, sa_d, sb_d


def _moe_block(x1, h2, aff, mods3, w_in, w_out, layer, ln_g, ln_b, B, S):
    E = N_EXPERTS
    cap = max(1, EC_CAPACITY_FACTOR * S // E)
    idx, gate, pos, off = _topk(aff, cap)
    gate_col = gate[:, :, 0, :].transpose(1, 0, 2).reshape(E, B * cap, 1)
    xe = _dispatch(idx.reshape(B, E, cap, 1), h2, B, S, cap)
    y = _moe_ffn(xe, gate_col, w_in, w_out, layer)
    window = min(COMBINE_WINDOW, cap)
    pos_t = pos.reshape(B, E, S).transpose(0, 2, 1)
    win = jnp.minimum(off[:, :, :, 0] // SLOT_ALIGN * SLOT_ALIGN, cap - window)
    win = win.transpose(0, 2, 1).reshape(-1)
    return _combine_ln(pos_t, win, y, x1, mods3, ln_g, ln_b, B, S, cap, window)


def kernel(x, c, positions, ada_w, ada_b, ln1_g, ln1_b, ln2_g, ln2_b, mla_w_in, mla_q_norm, mla_kv_norm,
           mla_w_uq, mla_w_ukv, mla_w_o, diff_w_in, diff_lambda, diff_subln, diff_w_o, router_w, moe_w_in,
           moe_w_out):
    B, S, D = x.shape
    cos_m, sin_m, c_d, sa_d, sb_d = _rope_tables(positions)
    mods = _mods(c, ada_w, ada_b)
    x2 = x.reshape(B * S, D)
    for i in range(DEPTH):
        mods3 = mods[i, :B].reshape(B, 6, D)
        j = i // 2
        if i % 2 == 0:
            q, k, v = _mla_proj(x2, mods3, cos_m, sin_m, mla_w_in[j], mla_q_norm[j], mla_kv_norm[j],
                                mla_w_uq[j], mla_w_ukv[j], B, S)
            o = _attention(_mla_attn_kernel, [], q, k, v, MLA_Q_TILE, "mla_attn")
            w_o = mla_w_o[j]
        else:
            lambda_init = 0.8 - 0.6 * math.exp(-0.3 * i)
            q, k, v = _diff_proj(x2, mods3, c_d, sa_d, sb_d, diff_w_in[j], B, S)
            o = _attention(functools.partial(_diff_attn_kernel, lambda_init=lambda_init),
                           [diff_lambda[j], diff_subln[j].reshape(1, DIFF_V)], q, k, v, DIFF_Q_TILE,
                           "diff_attn")
            w_o = diff_w_o[j]
        x1, h2, aff = _post_attn(o.reshape(B * S, D), x2, mods3, w_o, ln1_g[i], ln1_b[i], router_w[i], B, S)
        x2 = _moe_block(x1, h2, aff, mods3, moe_w_in, moe_w_out, i, ln2_g[i], ln2_b[i], B, S)
    return x2.reshape(B, S, D)
```

```python
import functools
import math

import jax
import jax.numpy as jnp
from jax import lax
from jax.experimental import pallas as pl
from jax.experimental.pallas import tpu as pltpu

F32 = jnp.float32
BF16 = jnp.bfloat16

D_MODEL = 1024
DEPTH = 2
MLA_HEADS = 8
MLA_Q_LORA = 384
MLA_KV_LORA = 256
MLA_NOPE = 128
MLA_ROPE = 64
MLA_V = 128
MLA_THETA = 10000.0
DIFF_HEADS = 8
DIFF_HEAD_DIM = 64
DIFF_V = 2 * DIFF_HEAD_DIM
DIFF_ROT = DIFF_HEAD_DIM // 4
ROPE_THETA = 500000.0
N_EXPERTS = 16
EXPERT_FF = 2048
EC_CAPACITY_FACTOR = 2
NORM_EPS = 1e-5
LATENT_EPS = 1e-6
ALPHA = (2 * DEPTH) ** 0.25

LANES = 128
VMEM_LIMIT = 56 * 1024 * 1024

TOKEN_TILE = 512
KEY_CHUNK = 512
ROW_BLOCK = 256
LOG2E = math.log2(math.e)
FF_TILE = 512
SLOT_ALIGN = 16
COMBINE_WINDOW = 256
DISPATCH_SLOT_BLOCK = 128
DISPATCH_TOKEN_TILE = 256


def _cparams(*sem):
    return pltpu.CompilerParams(dimension_semantics=sem, vmem_limit_bytes=VMEM_LIMIT)


def _dot(a, b):
    return jnp.dot(a, b, preferred_element_type=F32)


def _split_bf16(a):
    hi = a.astype(BF16)
    lo = (a - hi.astype(F32)).astype(BF16)
    return hi, lo


def _dot3(a, b):
    ah, al = _split_bf16(a)
    bh, bl = _split_bf16(b)
    return _dot(ah, bh) + (_dot(ah, bl) + _dot(al, bh))


def _layer_norm(y, g, b):
    mu = jnp.mean(y, axis=-1, keepdims=True)
    yc = y - mu
    var = jnp.mean(yc * yc, axis=-1, keepdims=True)
    return yc * lax.rsqrt(var + NORM_EPS) * g + b


def _mods_kernel(c_ref, w_ref, b_ref, o_ref):
    c = c_ref[...]
    a = c * (1.0 / (1.0 + jnp.exp(-c)))
    o_ref[0] = _dot3(a, w_ref[0]) + b_ref[0]


def _mods(c, ada_w, ada_b):
    B, D = c.shape
    L, _, N = ada_w.shape
    rows = 8
    cp = jnp.zeros((rows, D), F32).at[:B].set(c)
    tn = 1536
    return pl.pallas_call(
        _mods_kernel,
        grid=(L, N // tn),
        in_specs=[
            pl.BlockSpec((rows, D), lambda l, j: (0, 0)),
            pl.BlockSpec((1, D, tn), lambda l, j: (l, 0, j)),
            pl.BlockSpec((1, 1, tn), lambda l, j: (l, 0, j)),
        ],
        out_specs=pl.BlockSpec((1, rows, tn), lambda l, j: (l, 0, j)),
        out_shape=jax.ShapeDtypeStruct((L, rows, N), F32),
        compiler_params=_cparams("parallel", "parallel"),
        name="adaln_mods",
    )(cp, ada_w, ada_b.reshape(L, 1, N))


def _mla_proj_kernel(x_ref, mod_ref, cos_ref, sin_ref, win_ref, qn_ref, kvn_ref, wq_ref, wkv_ref,
                     q_ref, k_ref, v_ref, *, scale):
    m = mod_ref[0]
    h = (x_ref[...] * (1.0 + m[1:2]) + m[0:1]).astype(BF16)
    z = _dot(h, win_ref[...])
    cq = z[:, :MLA_Q_LORA]
    ckv = z[:, MLA_Q_LORA:MLA_Q_LORA + MLA_KV_LORA]
    kr = z[:, 640:768]
    krs = z[:, 768:896]
    cos = cos_ref[...]
    sin = sin_ref[...]
    qn = (cq * lax.rsqrt(jnp.mean(cq * cq, axis=-1, keepdims=True) + LATENT_EPS) * qn_ref[...]).astype(BF16)
    kvn = (ckv * lax.rsqrt(jnp.mean(ckv * ckv, axis=-1, keepdims=True) + LATENT_EPS) * kvn_ref[...]).astype(BF16)
    q = _dot(qn, wq_ref[...])
    kv = _dot(kvn, wkv_ref[...])
    kpe = (kr * cos + krs * sin).astype(BF16)
    hw = MLA_HEADS * LANES
    for hh in range(MLA_HEADS):
        a, b = hh * LANES, (hh + 1) * LANES
        q_ref[0, hh, :, 0:LANES] = (q[:, a:b] * scale).astype(BF16)
        q_ref[0, hh, :, LANES:2 * LANES] = (
            (q[:, hw + a:hw + b] * cos + q[:, 2 * hw + a:2 * hw + b] * sin) * scale).astype(BF16)
        k_ref[0, hh, :, 0:LANES] = kv[:, a:b].astype(BF16)
        k_ref[0, hh, :, LANES:2 * LANES] = kpe
        v_ref[0, hh] = kv[:, hw + a:hw + b].astype(BF16)


def _mla_proj(x2, mods3, cos_t, sin_t, w_in, q_norm, kv_norm, w_uq, w_ukv, B, S):
    T, D = x2.shape
    H = MLA_HEADS
    tm = min(TOKEN_TILE, S)
    npb = S // tm
    half = MLA_ROPE // 2
    kr = w_in[:, 640:704]
    kr_sw = jnp.concatenate([-kr[:, half:], kr[:, :half]], axis=1)
    z64 = jnp.zeros((D, 64), F32)
    win_p = jnp.concatenate([w_in[:, :640], kr, z64, kr_sw, z64], axis=1).astype(BF16)
    wq3 = w_uq.reshape(MLA_Q_LORA, H, MLA_NOPE + MLA_ROPE)
    rope = wq3[:, :, MLA_NOPE:]
    rope_sw = jnp.concatenate([-rope[:, :, half:], rope[:, :, :half]], axis=2)
    pad = lambda t: jnp.pad(t, ((0, 0), (0, 0), (0, LANES - MLA_ROPE))).reshape(MLA_Q_LORA, H * LANES)
    wq_p = jnp.concatenate([wq3[:, :, :MLA_NOPE].reshape(MLA_Q_LORA, H * MLA_NOPE), pad(rope), pad(rope_sw)],
                           axis=1).astype(BF16)
    wkv3 = w_ukv.reshape(MLA_KV_LORA, H, MLA_NOPE + MLA_V)
    wkv_p = jnp.concatenate([wkv3[:, :, :MLA_NOPE].reshape(MLA_KV_LORA, H * MLA_NOPE),
                             wkv3[:, :, MLA_NOPE:].reshape(MLA_KV_LORA, H * MLA_V)], axis=1).astype(BF16)
    scale = (MLA_NOPE + MLA_ROPE) ** -0.5 * LOG2E
    full = lambda shape: pl.BlockSpec(shape, lambda i: (0,) * len(shape))
    return pl.pallas_call(
        functools.partial(_mla_proj_kernel, scale=scale),
        grid=(T // tm,),
        in_specs=[
            pl.BlockSpec((tm, D), lambda i: (i, 0)),
            pl.BlockSpec((1, 6, D), lambda i: (i // npb, 0, 0)),
            pl.BlockSpec((tm, LANES), lambda i: (i, 0)),
            pl.BlockSpec((tm, LANES), lambda i: (i, 0)),
            full(win_p.shape),
            full((1, MLA_Q_LORA)),
            full((1, MLA_KV_LORA)),
            full(wq_p.shape),
            full(wkv_p.shape),
        ],
        out_specs=[
            pl.BlockSpec((1, H, tm, 2 * LANES), lambda i: (i // npb, 0, i % npb, 0)),
            pl.BlockSpec((1, H, tm, 2 * LANES), lambda i: (i // npb, 0, i % npb, 0)),
            pl.BlockSpec((1, H, tm, MLA_V), lambda i: (i // npb, 0, i % npb, 0)),
        ],
        out_shape=[
            jax.ShapeDtypeStruct((B, H, S, 2 * LANES), BF16),
            jax.ShapeDtypeStruct((B, H, S, 2 * LANES), BF16),
            jax.ShapeDtypeStruct((B, H, S, MLA_V), BF16),
        ],
        compiler_params=_cparams("parallel"),
        name="mla_proj",
    )(x2, mods3, cos_t, sin_t, win_p, q_norm.reshape(1, -1), kv_norm.reshape(1, -1), wq_p, wkv_p)


_NT = (((1,), (1,)), ((), ()))


def _attn_pipeline(load_q, n_maps, k_ref, vt_ref, st_ref, m_ref, emit, n_blocks):
    S = k_ref.shape[2]
    tk = min(KEY_CHUNK, S)
    chunks = [slice(c * tk, (c + 1) * tk) for c in range(S // tk)]

    def scores(blk, slot):
        for mp, qb in enumerate(load_q(blk)):
            m = None
            for ck in chunks:
                sj = lax.dot_general(k_ref[0, 0, ck, :], qb, _NT, preferred_element_type=F32)
                st_ref[slot, mp, ck, :] = sj
                mj = jnp.max(sj, axis=0, keepdims=True)
                m = mj if m is None else jnp.maximum(m, mj)
            m_ref[slot, mp] = m

    def attend(blk, slot):
        res = []
        for mp in range(n_maps):
            m = m_ref[slot, mp]
            l = None
            acc = None
            for ck in chunks:
                e = jnp.exp2(st_ref[slot, mp, ck, :] - m)
                lj = jnp.sum(e, axis=0, keepdims=True)
                l = lj if l is None else l + lj
                pv = _dot(vt_ref[:, ck], e.astype(BF16))
                acc = pv if acc is None else acc + pv
            res.append((acc, l))
        emit(blk, res)

    scores(0, 0)

    def body(jj, carry):
        j = 2 * jj
        scores(j + 1, 1)
        attend(j, 0)
        scores(jnp.minimum(j + 2, n_blocks - 1), 0)
        attend(j + 1, 1)
        return carry

    lax.fori_loop(0, n_blocks // 2, body, 0)


def _block_rows(blk):
    return pl.ds(pl.multiple_of(blk * ROW_BLOCK, ROW_BLOCK), ROW_BLOCK)


def _mla_attn_kernel(q_ref, k_ref, v_ref, o_ref, vt_ref, st_ref, m_ref):
    vt_ref[...] = v_ref[0, 0].astype(F32).T.astype(BF16)

    def emit(blk, res):
        (acc, l), = res
        o_ref[0, _block_rows(blk), :] = (acc * (1.0 / l)).T.astype(BF16)

    _attn_pipeline(lambda blk: [q_ref[0, 0, _block_rows(blk), :]], 1, k_ref, vt_ref, st_ref, m_ref, emit,
                   q_ref.shape[2] // ROW_BLOCK)


def _diff_attn_kernel(lam_ref, subln_ref, q_ref, k_ref, v_ref, o_ref, vt_ref, st_ref, m_ref, *, lambda_init):
    vt_ref[...] = v_ref[0, 0].astype(F32).T.astype(BF16)
    lf = lam_ref[...]
    lam = (jnp.exp(jnp.sum(lf[0:1] * lf[1:2], axis=-1, keepdims=True))
           - jnp.exp(jnp.sum(lf[2:3] * lf[3:4], axis=-1, keepdims=True)) + lambda_init)

    def load_q(blk):
        q = q_ref[0, 0, _block_rows(blk), :]
        lane = lax.broadcasted_iota(jnp.int32, q.shape, 1)
        zero = jnp.zeros_like(q)
        return [jnp.where(lane < DIFF_HEAD_DIM, q, zero), jnp.where(lane >= DIFF_HEAD_DIM, q, zero)]

    def emit(blk, res):
        (acc1, l1), (acc2, l2) = res
        o = acc1 * (1.0 / l1) - acc2 * (lam / l2)
        o = o * lax.rsqrt(jnp.mean(o * o, axis=0, keepdims=True) + NORM_EPS) * subln_ref[...]
        o_ref[0, _block_rows(blk), :] = (o * (1.0 - lambda_init)).T.astype(BF16)

    _attn_pipeline(load_q, 2, k_ref, vt_ref, st_ref, m_ref, emit, q_ref.shape[2] // ROW_BLOCK)


def _attention(kernel_fn, extra, n_maps, q, k, v, name):
    B, H, S, dq = q.shape
    dv = v.shape[-1]
    assert (S // ROW_BLOCK) % 2 == 0
    extra_specs = [pl.BlockSpec(a.shape, lambda b, h, n=a.ndim: (0,) * n) for a in extra]
    return pl.pallas_call(
        kernel_fn,
        grid=(B, H),
        in_specs=extra_specs + [
            pl.BlockSpec((1, 1, S, dq), lambda b, h: (b, h, 0, 0)),
            pl.BlockSpec((1, 1, S, dq), lambda b, h: (b, h, 0, 0)),
            pl.BlockSpec((1, 1, S, dv), lambda b, h: (b, h, 0, 0)),
        ],
        out_specs=pl.BlockSpec((1, S, dv), lambda b, h: (b, 0, h)),
        out_shape=jax.ShapeDtypeStruct((B, S, H * dv), BF16),
        scratch_shapes=[
            pltpu.VMEM((dv, S), BF16),
            pltpu.VMEM((2, n_maps, S, ROW_BLOCK), F32),
            pltpu.VMEM((2, n_maps, 1, ROW_BLOCK), F32),
        ],
        compiler_params=_cparams("parallel", "parallel"),
        name=name,
    )(*extra, q, k, v)


def _diff_proj_kernel(x_ref, mod_ref, c_ref, sa_ref, sb_ref, win_ref, q_ref, k_ref, v_ref, *, scale):
    m = mod_ref[0]
    h = (x_ref[...] * (1.0 + m[1:2]) + m[0:1]).astype(BF16)
    z = _dot(h, win_ref[...])
    c = c_ref[...]
    sa = sa_ref[...]
    sb = sb_ref[...]
    hw = DIFF_HEADS * LANES

    def rope(t):
        return t * c + pltpu.roll(t, LANES - DIFF_ROT // 2, 1) * sa + pltpu.roll(t, DIFF_ROT // 2, 1) * sb

    for hh in range(DIFF_HEADS):
        a, b = hh * LANES, (hh + 1) * LANES
        q_ref[0, hh] = (rope(z[:, a:b]) * scale).astype(BF16)
        k_ref[0, hh] = rope(z[:, hw + a:hw + b]).astype(BF16)
        v_ref[0, hh] = z[:, 2 * hw + a:2 * hw + b].astype(BF16)


def _diff_proj(x2, mods3, c_t, sa_t, sb_t, w_in, B, S):
    T, D = x2.shape
    H = DIFF_HEADS
    tm = min(TOKEN_TILE, S)
    npb = S // tm
    win = w_in.astype(BF16)
    hd = jax.ShapeDtypeStruct((B, H, S, LANES), BF16)
    hspec = pl.BlockSpec((1, H, tm, LANES), lambda i: (i // npb, 0, i % npb, 0))
    return pl.pallas_call(
        functools.partial(_diff_proj_kernel, scale=DIFF_HEAD_DIM ** -0.5 * LOG2E),
        grid=(T // tm,),
        in_specs=[
            pl.BlockSpec((tm, D), lambda i: (i, 0)),
            pl.BlockSpec((1, 6, D), lambda i: (i // npb, 0, 0)),
            pl.BlockSpec((tm, LANES), lambda i: (i, 0)),
            pl.BlockSpec((tm, LANES), lambda i: (i, 0)),
            pl.BlockSpec((tm, LANES), lambda i: (i, 0)),
            pl.BlockSpec(win.shape, lambda i: (0, 0)),
        ],
        out_specs=[hspec, hspec, hspec],
        out_shape=[hd, hd, hd],
        compiler_params=_cparams("parallel"),
        name="diff_proj",
    )(x2, mods3, c_t, sa_t, sb_t, win)


def _post_attn_kernel(o_ref, x_ref, mod_ref, wo_ref, g_ref, b_ref, rw_ref, x1_ref, h2_ref, aff_ref):
    m = mod_ref[0]
    t = _dot(o_ref[...], wo_ref[...])
    x1 = _layer_norm(ALPHA * x_ref[...] + (1.0 + m[2:3]) * t, g_ref[...], b_ref[...])
    x1_ref[...] = x1
    h2 = x1 * (1.0 + m[4:5]) + m[3:4]
    h2_ref[...] = h2
    logits = _dot3(h2, rw_ref[...])
    lt = logits.T[:N_EXPERTS]
    mx = jnp.max(lt, axis=0, keepdims=True)
    ex = jnp.exp(lt - mx)
    aff_ref[0] = ex / jnp.sum(ex, axis=0, keepdims=True)


def _post_attn(o2, x2, mods3, w_o, ln_g, ln_b, router_w, B, S):
    T, D = x2.shape
    E = N_EXPERTS
    tm = min(TOKEN_TILE, S)
    npb = S // tm
    rw = jnp.pad(router_w, ((0, 0), (0, LANES - E)))
    return pl.pallas_call(
        _post_attn_kernel,
        grid=(T // tm,),
        in_specs=[
            pl.BlockSpec((tm, D), lambda i: (i, 0)),
            pl.BlockSpec((tm, D), lambda i: (i, 0)),
            pl.BlockSpec((1, 6, D), lambda i: (i // npb, 0, 0)),
            pl.BlockSpec((D, D), lambda i: (0, 0)),
            pl.BlockSpec((1, D), lambda i: (0, 0)),
            pl.BlockSpec((1, D), lambda i: (0, 0)),
            pl.BlockSpec((D, LANES), lambda i: (0, 0)),
        ],
        out_specs=[
            pl.BlockSpec((tm, D), lambda i: (i, 0)),
            pl.BlockSpec((tm, D), lambda i: (i, 0)),
            pl.BlockSpec((1, E, tm), lambda i: (i // npb, 0, i % npb)),
        ],
        out_shape=[
            jax.ShapeDtypeStruct((T, D), F32),
            jax.ShapeDtypeStruct((T, D), F32),
            jax.ShapeDtypeStruct((B, E, S), F32),
        ],
        compiler_params=_cparams("parallel"),
        name="post_attn",
    )(o2, x2, mods3, w_o.astype(BF16), ln_g.reshape(1, D), ln_b.reshape(1, D), rw)


def _topk_kernel(aff_ref, idx_ref, gate_ref, pos_ref, off_ref, *, cap):
    a3 = aff_ref[0]
    E, NC, _ = a3.shape
    R = E * NC

    def count3(mask3):
        c = jnp.sum(jnp.where(mask3, 1.0, 0.0), axis=2, keepdims=True)
        return jnp.sum(c, axis=1, keepdims=True)

    def body(i, bits):
        cand = bits | jnp.left_shift(jnp.int32(1), 30 - i)
        reach = count3(a3 >= lax.bitcast_convert_type(cand, F32))
        return jnp.where(reach >= cap, cand, bits)

    thr = lax.bitcast_convert_type(lax.fori_loop(0, 31, body, jnp.zeros((E, 1, 1), jnp.int32)), F32)
    gt3 = a3 > thr
    eq3 = a3 == thr
    need = cap - count3(gt3)

    li = lax.broadcasted_iota(jnp.int32, (LANES, LANES), 0)
    lj = lax.broadcasted_iota(jnp.int32, (LANES, LANES), 1)
    tri_excl = jnp.where(li < lj, 1.0, 0.0).astype(BF16)
    tri_incl = jnp.where(li <= lj, 1.0, 0.0).astype(BF16)
    ri = lax.broadcasted_iota(jnp.int32, (R, R), 0)
    rj = lax.broadcasted_iota(jnp.int32, (R, R), 1)
    same = (ri // NC) == (rj // NC)
    blk_lower = jnp.where(same & (rj < ri), 1.0, 0.0).astype(BF16)

    def chunk_offsets(mask2):
        tot = jnp.sum(mask2, axis=1, keepdims=True)
        totb = jnp.broadcast_to(tot, (R, LANES)).astype(BF16)
        return _dot(blk_lower, totb), tot

    eq2 = jnp.where(eq3, 1.0, 0.0).reshape(R, LANES)
    eq_off, _ = chunk_offsets(eq2)
    eq_rank = (eq_off + _dot(eq2.astype(BF16), tri_excl)).reshape(E, NC, LANES)
    sel3 = gt3 | (eq3 & (eq_rank < need))
    sel2 = jnp.where(sel3, 1.0, 0.0).reshape(R, LANES)
    off2, tot2 = chunk_offsets(sel2)
    cum2 = _dot(sel2.astype(BF16), tri_incl)
    pos_ref[0] = jnp.where(sel3, (off2 + cum2 - 1.0).reshape(E, NC, LANES), -1.0).astype(jnp.int32)
    off_ref[0] = off2.reshape(E, NC, LANES).astype(jnp.int32)
    a2 = a3.reshape(R, LANES)
    a_hi = a2.astype(BF16)
    a_mid = (a2 - a_hi.astype(F32)).astype(BF16)
    a_lo = (a2 - a_hi.astype(F32) - a_mid.astype(F32)).astype(BF16)

    cs = lax.broadcasted_iota(jnp.int32, (NC, cap), 1).astype(F32)
    ks = lax.broadcasted_iota(jnp.int32, (NC, cap), 0).astype(F32)
    ls = lax.broadcasted_iota(jnp.int32, (LANES, cap), 0).astype(F32)
    for e in range(E):
        rows = slice(e * NC, (e + 1) * NC)
        off_e = off2[rows, 0:1]
        end_e = off_e + tot2[rows]
        k_of_c = jnp.sum(jnp.where(end_e <= cs, 1.0, 0.0), axis=0, keepdims=True)
        onehot = ks == k_of_c
        oh = jnp.where(onehot, 1.0, 0.0).astype(BF16)
        off_c = jnp.sum(jnp.where(onehot, off_e, 0.0), axis=0, keepdims=True)
        rank_c = cs[0:1] - off_c
        cum_c = _dot(cum2[rows].T.astype(BF16), oh)
        lane_c = jnp.sum(jnp.where(cum_c <= rank_c, 1.0, 0.0), axis=0, keepdims=True)
        idx_ref[0, e] = (k_of_c * LANES + lane_c).astype(jnp.int32)
        g_c = (_dot(a_hi[rows].T, oh) + _dot(a_mid[rows].T, oh)) + _dot(a_lo[rows].T, oh)
        gate_ref[0, e] = jnp.sum(jnp.where(ls == lane_c, g_c, 0.0), axis=0, keepdims=True)


def _topk(aff, cap):
    B, E, S = aff.shape
    NC = S // LANES
    slot_spec = pl.BlockSpec((1, E, 1, cap), lambda b: (b, 0, 0, 0))
    tok_spec = pl.BlockSpec((1, E, NC, LANES), lambda b: (b, 0, 0, 0))
    return pl.pallas_call(
        functools.partial(_topk_kernel, cap=cap),
        grid=(B,),
        in_specs=[tok_spec],
        out_specs=[slot_spec, slot_spec, tok_spec, tok_spec],
        out_shape=[
            jax.ShapeDtypeStruct((B, E, 1, cap), jnp.int32),
            jax.ShapeDtypeStruct((B, E, 1, cap), F32),
            jax.ShapeDtypeStruct((B, E, NC, LANES), jnp.int32),
            jax.ShapeDtypeStruct((B, E, NC, LANES), jnp.int32),
        ],
        compiler_params=_cparams("parallel"),
        name="ec_topk",
    )(aff.reshape(B, E, NC, LANES))


def _dispatch_kernel(idx_ref, h_ref, xe_ref):
    cap = xe_ref.shape[1]

    def group(g, carry):
        rows = [h_ref[pl.ds(idx_ref[0, 0, g * SLOT_ALIGN + r], 1), :] for r in range(SLOT_ALIGN)]
        start = pl.multiple_of(g * SLOT_ALIGN, SLOT_ALIGN)
        xe_ref[0, pl.ds(start, SLOT_ALIGN), :] = jnp.concatenate(rows, axis=0).astype(BF16)
        return carry

    lax.fori_loop(0, cap // SLOT_ALIGN, group, 0)


def _dispatch(idx, h2, B, S, cap):
    E = idx.shape[1]
    D = h2.shape[1]
    return pl.pallas_call(
        _dispatch_kernel,
        grid=(B, E),
        in_specs=[
            pl.BlockSpec((1, 1, cap), lambda b, e: (b * E + e, 0, 0), memory_space=pltpu.SMEM),
            pl.BlockSpec((S, D), lambda b, e: (b, 0)),
        ],
        out_specs=pl.BlockSpec((1, cap, D), lambda b, e: (e, b, 0)),
        out_shape=jax.ShapeDtypeStruct((E, B * cap, D), BF16),
        compiler_params=_cparams("parallel", "parallel"),
        name="moe_dispatch",
    )(idx.reshape(B * E, 1, cap), h2)


def _moe_ffn_kernel(xe_ref, gate_ref, wg_ref, wu_ref, wo_ref, y_ref, acc_ref):
    f = pl.program_id(1)

    @pl.when(f == 0)
    def _():
        acc_ref[...] = jnp.zeros_like(acc_ref)

    xe = xe_ref[0]
    g = _dot(xe, wg_ref[0, 0].astype(BF16))
    u = _dot(xe, wu_ref[0, 0].astype(BF16))
    act = (g * (1.0 / (1.0 + jnp.exp(-g))) * u).astype(BF16)
    acc_ref[...] += _dot(act, wo_ref[0, 0].astype(BF16))

    @pl.when(f == pl.num_programs(1) - 1)
    def _():
        y_ref[0] = (acc_ref[...] * gate_ref[0]).astype(BF16)


def _moe_ffn(xe, gate_col, w_in, w_out, layer):
    E, M, D = xe.shape
    FF = w_out.shape[2]
    tf = FF_TILE
    nf = FF // tf
    return pl.pallas_call(
        _moe_ffn_kernel,
        grid=(E, nf),
        in_specs=[
            pl.BlockSpec((1, M, D), lambda e, f: (e, 0, 0)),
            pl.BlockSpec((1, M, 1), lambda e, f: (e, 0, 0)),
            pl.BlockSpec((1, 1, D, tf), lambda e, f: (layer, e, 0, f)),
            pl.BlockSpec((1, 1, D, tf), lambda e, f: (layer, e, 0, nf + f)),
            pl.BlockSpec((1, 1, tf, D), lambda e, f: (layer, e, f, 0)),
        ],
        out_specs=pl.BlockSpec((1, M, D), lambda e, f: (e, 0, 0)),
        out_shape=jax.ShapeDtypeStruct((E, M, D), BF16),
        scratch_shapes=[pltpu.VMEM((M, D), F32)],
        compiler_params=_cparams("parallel", "arbitrary"),
        name="moe_ffn",
    )(xe, gate_col, w_in, w_in, w_out)


def _combine_ln_kernel(win_ref, pos_ref, y_ref, x_ref, mod_ref, g_ref, b_ref, o_ref, *, window):
    b = pl.program_id(0)
    j = pl.program_id(1)
    n_exp = y_ref.shape[0]
    pos = pos_ref[0]
    slot = lax.broadcasted_iota(jnp.int32, (pos.shape[0], window), 1)
    f = None
    for e in range(n_exp):
        start = pl.multiple_of(win_ref[(b * pl.num_programs(1) + j) * n_exp + e], SLOT_ALIGN)
        onehot = jnp.where(pos[:, e:e + 1] - start == slot, 1.0, 0.0).astype(BF16)
        part = _dot(onehot, y_ref[e, pl.ds(start, window), :])
        f = part if f is None else f + part
    m = mod_ref[0]
    o_ref[...] = _layer_norm(ALPHA * x_ref[...] + (1.0 + m[5:6]) * f, g_ref[...], b_ref[...])


def _combine_ln(pos_t, win, y, x1, mods3, ln_g, ln_b, B, S, cap, window):
    E, _, D = y.shape
    nj = S // LANES
    row = pl.BlockSpec((LANES, D), lambda b, j, w: (b * nj + j, 0))
    vec = pl.BlockSpec((1, D), lambda b, j, w: (0, 0))
    return pl.pallas_call(
        functools.partial(_combine_ln_kernel, window=window),
        grid_spec=pltpu.PrefetchScalarGridSpec(
            num_scalar_prefetch=1,
            grid=(B, nj),
            in_specs=[
                pl.BlockSpec((1, LANES, E), lambda b, j, w: (b, j, 0)),
                pl.BlockSpec((E, cap, D), lambda b, j, w: (0, b, 0)),
                row,
                pl.BlockSpec((1, 6, D), lambda b, j, w: (b, 0, 0)),
                vec,
                vec,
            ],
            out_specs=row,
        ),
        out_shape=jax.ShapeDtypeStruct((B * S, D), F32),
        compiler_params=_cparams("parallel", "arbitrary"),
        name="moe_combine_ln",
    )(win, pos_t, y, x1, mods3, ln_g.reshape(1, D), ln_b.reshape(1, D))


def _rope_tables(positions):
    B, S = positions.shape
    pos = positions.astype(F32)[..., None]
    inv_m = MLA_THETA ** (-jnp.arange(0, MLA_ROPE, 2, dtype=F32) / MLA_ROPE)
    ang_m = (pos * inv_m).reshape(B * S, MLA_ROPE // 2)
    cos_m = jnp.tile(jnp.cos(ang_m), (1, LANES // (MLA_ROPE // 2)))
    sin_m = jnp.tile(jnp.sin(ang_m), (1, LANES // (MLA_ROPE // 2)))
    inv_d = ROPE_THETA ** (-jnp.arange(0, DIFF_ROT, 2, dtype=F32) / DIFF_ROT)
    ang_d = (pos * inv_d).reshape(B * S, DIFF_ROT // 2)
    cd, sd = jnp.cos(ang_d), jnp.sin(ang_d)
    hr = DIFF_ROT // 2
    rest = DIFF_HEAD_DIM - DIFF_ROT
    ones = jnp.ones((B * S, rest), F32)
    zeros = jnp.zeros((B * S, rest), F32)
    zr = jnp.zeros((B * S, hr), F32)
    rep = LANES // DIFF_HEAD_DIM
    c_d = jnp.tile(jnp.concatenate([cd, cd, ones], axis=1), (1, rep))
    sa_d = jnp.tile(jnp.concatenate([-sd, zr, zeros], axis=1), (1, rep))
    sb_d = jnp.tile(jnp.concatenate([zr, sd, zeros], axis=1), (1, rep))
    return cos_m, sin_m, c_d, sa_d, sb_d


def _moe_block(x1, h2, aff, mods3, w_in, w_out, layer, ln_g, ln_b, B, S):
    E = N_EXPERTS
    cap = max(1, EC_CAPACITY_FACTOR * S // E)
    idx, gate, pos, off = _topk(aff, cap)
    gate_col = gate[:, :, 0, :].transpose(1, 0, 2).reshape(E, B * cap, 1)
    xe = _dispatch(idx, h2, B, S, cap)
    y = _moe_ffn(xe, gate_col, w_in, w_out, layer)
    window = min(COMBINE_WINDOW, cap)
    pos_t = pos.reshape(B, E, S).transpose(0, 2, 1)
    win = jnp.minimum(off[:, :, :, 0] // SLOT_ALIGN * SLOT_ALIGN, cap - window)
    win = win.transpose(0, 2, 1).reshape(-1)
    return _combine_ln(pos_t, win, y, x1, mods3, ln_g, ln_b, B, S, cap, window)


def kernel(x, c, positions, ada_w, ada_b, ln1_g, ln1_b, ln2_g, ln2_b, mla_w_in, mla_q_norm, mla_kv_norm,
           mla_w_uq, mla_w_ukv, mla_w_o, diff_w_in, diff_lambda, diff_subln, diff_w_o, router_w, moe_w_in,
           moe_w_out):
    B, S, D = x.shape
    cos_m, sin_m, c_d, sa_d, sb_d = _rope_tables(positions)
    mods = _mods(c, ada_w, ada_b)
    x2 = x.reshape(B * S, D)
    for i in range(DEPTH):
        mods3 = mods[i, :B].reshape(B, 6, D)
        j = i // 2
        if i % 2 == 0:
            q, k, v = _mla_proj(x2, mods3, cos_m, sin_m, mla_w_in[j], mla_q_norm[j], mla_kv_norm[j],
                                mla_w_uq[j], mla_w_ukv[j], B, S)
            o = _attention(_mla_attn_kernel, [], 1, q, k, v, "mla_attn")
            w_o = mla_w_o[j]
        else:
            lambda_init = 0.8 - 0.6 * math.exp(-0.3 * i)
            q, k, v = _diff_proj(x2, mods3, c_d, sa_d, sb_d, diff_w_in[j], B, S)
            o = _attention(functools.partial(_diff_attn_kernel, lambda_init=lambda_init),
                           [diff_lambda[j], diff_subln[j].reshape(DIFF_V, 1)], 2, q, k, v, "diff_attn")
            w_o = diff_w_o[j]
        x1, h2, aff = _post_attn(o.reshape(B * S, D), x2, mods3, w_o, ln1_g[i], ln1_b[i], router_w[i], B, S)
        x2 = _moe_block(x1, h2, aff, mods3, moe_w_in, moe_w_out, i, ln2_g[i], ln2_b[i], B, S)
    return x2.reshape(B, S, D)
```

```python
import functools
import math

import jax
import jax.numpy as jnp
from jax import lax
from jax.experimental import pallas as pl
from jax.experimental.pallas import tpu as pltpu

F32 = jnp.float32
BF16 = jnp.bfloat16

D_MODEL = 1024
DEPTH = 2
MLA_HEADS = 8
MLA_Q_LORA = 384
MLA_KV_LORA = 256
MLA_NOPE = 128
MLA_ROPE = 64
MLA_V = 128
MLA_THETA = 10000.0
DIFF_HEADS = 8
DIFF_HEAD_DIM = 64
DIFF_V = 2 * DIFF_HEAD_DIM
DIFF_ROT = DIFF_HEAD_DIM // 4
ROPE_THETA = 500000.0
N_EXPERTS = 16
EXPERT_FF = 2048
EC_CAPACITY_FACTOR = 2
NORM_EPS = 1e-5
LATENT_EPS = 1e-6
ALPHA = (2 * DEPTH) ** 0.25

LANES = 128
VMEM_LIMIT = 56 * 1024 * 1024

TOKEN_TILE = 512
POST_TILE = 1024
POST_SUB_ROWS = 256
KEY_CHUNK = 512
ROW_BLOCK = 256
ATTN_UNROLL = 2
LOG2E = math.log2(math.e)
FF_TILE = 512
SLOT_ALIGN = 16
COMBINE_WINDOW = 256
COMBINE_TILE = 256


def _cparams(*sem):
    return pltpu.CompilerParams(dimension_semantics=sem, vmem_limit_bytes=VMEM_LIMIT)


def _dot(a, b):
    return jnp.dot(a, b, preferred_element_type=F32)


def _split_bf16(a):
    hi = a.astype(BF16)
    lo = (a - hi.astype(F32)).astype(BF16)
    return hi, lo


def _dot3(a, b):
    ah, al = _split_bf16(a)
    bh, bl = _split_bf16(b)
    return _dot(ah, bh) + (_dot(ah, bl) + _dot(al, bh))


def _layer_norm(y, g, b):
    mu = jnp.mean(y, axis=-1, keepdims=True)
    yc = y - mu
    var = jnp.mean(yc * yc, axis=-1, keepdims=True)
    return yc * lax.rsqrt(var + NORM_EPS) * g + b


def _mods_kernel(c_ref, w_ref, b_ref, o_ref):
    c = c_ref[...]
    a = c * (1.0 / (1.0 + jnp.exp(-c)))
    o_ref[0] = _dot3(a, w_ref[0]) + b_ref[0]


def _mods(c, ada_w, ada_b):
    B, D = c.shape
    L, _, N = ada_w.shape
    rows = 8
    cp = jnp.zeros((rows, D), F32).at[:B].set(c)
    tn = 1536
    return pl.pallas_call(
        _mods_kernel,
        grid=(L, N // tn),
        in_specs=[
            pl.BlockSpec((rows, D), lambda l, j: (0, 0)),
            pl.BlockSpec((1, D, tn), lambda l, j: (l, 0, j)),
            pl.BlockSpec((1, 1, tn), lambda l, j: (l, 0, j)),
        ],
        out_specs=pl.BlockSpec((1, rows, tn), lambda l, j: (l, 0, j)),
        out_shape=jax.ShapeDtypeStruct((L, rows, N), F32),
        compiler_params=_cparams("parallel", "parallel"),
        name="adaln_mods",
    )(cp, ada_w, ada_b.reshape(L, 1, N))


def _mla_proj_kernel(x_ref, mod_ref, cos_ref, sin_ref, win_ref, qn_ref, kvn_ref, wq_ref, wkv_ref,
                     q_ref, k_ref, v_ref, *, scale):
    m = mod_ref[0]
    h = (x_ref[...] * (1.0 + m[1:2]) + m[0:1]).astype(BF16)
    z = _dot(h, win_ref[...])
    cq = z[:, :MLA_Q_LORA]
    ckv = z[:, MLA_Q_LORA:MLA_Q_LORA + MLA_KV_LORA]
    kr = z[:, 640:768]
    krs = z[:, 768:896]
    cos = cos_ref[...]
    sin = sin_ref[...]
    qn = (cq * lax.rsqrt(jnp.mean(cq * cq, axis=-1, keepdims=True) + LATENT_EPS) * qn_ref[...]).astype(BF16)
    kvn = (ckv * lax.rsqrt(jnp.mean(ckv * ckv, axis=-1, keepdims=True) + LATENT_EPS) * kvn_ref[...]).astype(BF16)
    q = _dot(qn, wq_ref[...])
    kv = _dot(kvn, wkv_ref[...])
    kpe = (kr * cos + krs * sin).astype(BF16)
    hw = MLA_HEADS * LANES
    for hh in range(MLA_HEADS):
        a, b = hh * LANES, (hh + 1) * LANES
        q_ref[0, hh, :, 0:LANES] = (q[:, a:b] * scale).astype(BF16)
        q_ref[0, hh, :, LANES:2 * LANES] = (
            (q[:, hw + a:hw + b] * cos + q[:, 2 * hw + a:2 * hw + b] * sin) * scale).astype(BF16)
        k_ref[0, hh, :, 0:LANES] = kv[:, a:b].astype(BF16)
        k_ref[0, hh, :, LANES:2 * LANES] = kpe
        v_ref[0, hh] = kv[:, hw + a:hw + b].astype(BF16)


def _mla_proj(x2, mods3, cos_t, sin_t, w_in, q_norm, kv_norm, w_uq, w_ukv, B, S):
    T, D = x2.shape
    H = MLA_HEADS
    tm = min(TOKEN_TILE, S)
    npb = S // tm
    half = MLA_ROPE // 2
    kr = w_in[:, 640:704]
    kr_sw = jnp.concatenate([-kr[:, half:], kr[:, :half]], axis=1)
    z64 = jnp.zeros((D, 64), F32)
    win_p = jnp.concatenate([w_in[:, :640], kr, z64, kr_sw, z64], axis=1).astype(BF16)
    wq3 = w_uq.reshape(MLA_Q_LORA, H, MLA_NOPE + MLA_ROPE)
    rope = wq3[:, :, MLA_NOPE:]
    rope_sw = jnp.concatenate([-rope[:, :, half:], rope[:, :, :half]], axis=2)
    pad = lambda t: jnp.pad(t, ((0, 0), (0, 0), (0, LANES - MLA_ROPE))).reshape(MLA_Q_LORA, H * LANES)
    wq_p = jnp.concatenate([wq3[:, :, :MLA_NOPE].reshape(MLA_Q_LORA, H * MLA_NOPE), pad(rope), pad(rope_sw)],
                           axis=1).astype(BF16)
    wkv3 = w_ukv.reshape(MLA_KV_LORA, H, MLA_NOPE + MLA_V)
    wkv_p = jnp.concatenate([wkv3[:, :, :MLA_NOPE].reshape(MLA_KV_LORA, H * MLA_NOPE),
                             wkv3[:, :, MLA_NOPE:].reshape(MLA_KV_LORA, H * MLA_V)], axis=1).astype(BF16)
    scale = (MLA_NOPE + MLA_ROPE) ** -0.5 * LOG2E
    full = lambda shape: pl.BlockSpec(shape, lambda i: (0,) * len(shape))
    return pl.pallas_call(
        functools.partial(_mla_proj_kernel, scale=scale),
        grid=(T // tm,),
        in_specs=[
            pl.BlockSpec((tm, D), lambda i: (i, 0)),
            pl.BlockSpec((1, 6, D), lambda i: (i // npb, 0, 0)),
            pl.BlockSpec((tm, LANES), lambda i: (i, 0)),
            pl.BlockSpec((tm, LANES), lambda i: (i, 0)),
            full(win_p.shape),
            full((1, MLA_Q_LORA)),
            full((1, MLA_KV_LORA)),
            full(wq_p.shape),
            full(wkv_p.shape),
        ],
        out_specs=[
            pl.BlockSpec((1, H, tm, 2 * LANES), lambda i: (i // npb, 0, i % npb, 0)),
            pl.BlockSpec((1, H, tm, 2 * LANES), lambda i: (i // npb, 0, i % npb, 0)),
            pl.BlockSpec((1, H, tm, MLA_V), lambda i: (i // npb, 0, i % npb, 0)),
        ],
        out_shape=[
            jax.ShapeDtypeStruct((B, H, S, 2 * LANES), BF16),
            jax.ShapeDtypeStruct((B, H, S, 2 * LANES), BF16),
            jax.ShapeDtypeStruct((B, H, S, MLA_V), BF16),
        ],
        compiler_params=_cparams("parallel"),
        name="mla_proj",
    )(x2, mods3, cos_t, sin_t, win_p, q_norm.reshape(1, -1), kv_norm.reshape(1, -1), wq_p, wkv_p)


_NT = (((1,), (1,)), ((), ()))


def _attn_pipeline(load_q, n_maps, k_ref, vt_ref, st_ref, m_ref, emit, n_blocks):
    S = k_ref.shape[2]
    tk = min(KEY_CHUNK, S)
    chunks = [slice(c * tk, (c + 1) * tk) for c in range(S // tk)]

    def scores(blk, slot):
        for mp, qb in enumerate(load_q(blk)):
            m = None
            for ck in chunks:
                sj = lax.dot_general(k_ref[0, 0, ck, :], qb, _NT, preferred_element_type=F32)
                st_ref[slot, mp, ck, :] = sj
                mj = jnp.max(sj, axis=0, keepdims=True)
                m = mj if m is None else jnp.maximum(m, mj)
            m_ref[slot, mp] = m

    def attend(blk, slot):
        res = []
        for mp in range(n_maps):
            m = m_ref[slot, mp]
            l = None
            acc = None
            for ck in chunks:
                e = jnp.exp2(st_ref[slot, mp, ck, :] - m)
                lj = jnp.sum(e, axis=0, keepdims=True)
                l = lj if l is None else l + lj
                pv = _dot(vt_ref[:, ck], e.astype(BF16))
                acc = pv if acc is None else acc + pv
            res.append((acc, l))
        emit(blk, res)

    scores(0, 0)

    def body(jj, carry):
        j = ATTN_UNROLL * jj
        for u in range(ATTN_UNROLL):
            scores(jnp.minimum(j + u + 1, n_blocks - 1), (u + 1) % 2)
            attend(j + u, u % 2)
        return carry

    lax.fori_loop(0, n_blocks // ATTN_UNROLL, body, 0)


def _block_rows(blk):
    return pl.ds(pl.multiple_of(blk * ROW_BLOCK, ROW_BLOCK), ROW_BLOCK)


def _mla_attn_kernel(q_ref, k_ref, v_ref, o_ref, vt_ref, st_ref, m_ref):
    vt_ref[...] = v_ref[0, 0].astype(F32).T.astype(BF16)

    def emit(blk, res):
        (acc, l), = res
        o_ref[0, _block_rows(blk), :] = (acc * (1.0 / l)).T.astype(BF16)

    _attn_pipeline(lambda blk: [q_ref[0, 0, _block_rows(blk), :]], 1, k_ref, vt_ref, st_ref, m_ref, emit,
                   q_ref.shape[2] // ROW_BLOCK)


def _diff_attn_kernel(lam_ref, subln_ref, q_ref, k_ref, v_ref, o_ref, vt_ref, st_ref, m_ref, *, lambda_init):
    vt_ref[...] = v_ref[0, 0].astype(F32).T.astype(BF16)
    lf = lam_ref[...]
    lam = (jnp.exp(jnp.sum(lf[0:1] * lf[1:2], axis=-1, keepdims=True))
           - jnp.exp(jnp.sum(lf[2:3] * lf[3:4], axis=-1, keepdims=True)) + lambda_init)

    def load_q(blk):
        q = q_ref[0, 0, _block_rows(blk), :]
        lane = lax.broadcasted_iota(jnp.int32, q.shape, 1)
        zero = jnp.zeros_like(q)
        return [jnp.where(lane < DIFF_HEAD_DIM, q, zero), jnp.where(lane >= DIFF_HEAD_DIM, q, zero)]

    def emit(blk, res):
        (acc1, l1), (acc2, l2) = res
        o = acc1 * (1.0 / l1) - acc2 * (lam / l2)
        o = o * lax.rsqrt(jnp.mean(o * o, axis=0, keepdims=True) + NORM_EPS) * subln_ref[...]
        o_ref[0, _block_rows(blk), :] = (o * (1.0 - lambda_init)).T.astype(BF16)

    _attn_pipeline(load_q, 2, k_ref, vt_ref, st_ref, m_ref, emit, q_ref.shape[2] // ROW_BLOCK)


def _attention(kernel_fn, extra, n_maps, q, k, v, name):
    B, H, S, dq = q.shape
    dv = v.shape[-1]
    assert (S // ROW_BLOCK) % ATTN_UNROLL == 0 and ATTN_UNROLL % 2 == 0
    extra_specs = [pl.BlockSpec(a.shape, lambda b, h, n=a.ndim: (0,) * n) for a in extra]
    return pl.pallas_call(
        kernel_fn,
        grid=(B, H),
        in_specs=extra_specs + [
            pl.BlockSpec((1, 1, S, dq), lambda b, h: (b, h, 0, 0)),
            pl.BlockSpec((1, 1, S, dq), lambda b, h: (b, h, 0, 0)),
            pl.BlockSpec((1, 1, S, dv), lambda b, h: (b, h, 0, 0)),
        ],
        out_specs=pl.BlockSpec((1, S, dv), lambda b, h: (b, 0, h)),
        out_shape=jax.ShapeDtypeStruct((B, S, H * dv), BF16),
        scratch_shapes=[
            pltpu.VMEM((dv, S), BF16),
            pltpu.VMEM((2, n_maps, S, ROW_BLOCK), F32),
            pltpu.VMEM((2, n_maps, 1, ROW_BLOCK), F32),
        ],
        compiler_params=_cparams("parallel", "parallel"),
        name=name,
    )(*extra, q, k, v)


def _diff_proj_kernel(x_ref, mod_ref, c_ref, sa_ref, sb_ref, win_ref, q_ref, k_ref, v_ref, *, scale):
    m = mod_ref[0]
    h = (x_ref[...] * (1.0 + m[1:2]) + m[0:1]).astype(BF16)
    z = _dot(h, win_ref[...])
    c = c_ref[...]
    sa = sa_ref[...]
    sb = sb_ref[...]
    hw = DIFF_HEADS * LANES

    def rope(t):
        return t * c + pltpu.roll(t, LANES - DIFF_ROT // 2, 1) * sa + pltpu.roll(t, DIFF_ROT // 2, 1) * sb

    for hh in range(DIFF_HEADS):
        a, b = hh * LANES, (hh + 1) * LANES
        q_ref[0, hh] = (rope(z[:, a:b]) * scale).astype(BF16)
        k_ref[0, hh] = rope(z[:, hw + a:hw + b]).astype(BF16)
        v_ref[0, hh] = z[:, 2 * hw + a:2 * hw + b].astype(BF16)


def _diff_proj(x2, mods3, c_t, sa_t, sb_t, w_in, B, S):
    T, D = x2.shape
    H = DIFF_HEADS
    tm = min(TOKEN_TILE, S)
    npb = S // tm
    win = w_in.astype(BF16)
    hd = jax.ShapeDtypeStruct((B, H, S, LANES), BF16)
    hspec = pl.BlockSpec((1, H, tm, LANES), lambda i: (i // npb, 0, i % npb, 0))
    return pl.pallas_call(
        functools.partial(_diff_proj_kernel, scale=DIFF_HEAD_DIM ** -0.5 * LOG2E),
        grid=(T // tm,),
        in_specs=[
            pl.BlockSpec((tm, D), lambda i: (i, 0)),
            pl.BlockSpec((1, 6, D), lambda i: (i // npb, 0, 0)),
            pl.BlockSpec((tm, LANES), lambda i: (i, 0)),
            pl.BlockSpec((tm, LANES), lambda i: (i, 0)),
            pl.BlockSpec((tm, LANES), lambda i: (i, 0)),
            pl.BlockSpec(win.shape, lambda i: (0, 0)),
        ],
        out_specs=[hspec, hspec, hspec],
        out_shape=[hd, hd, hd],
        compiler_params=_cparams("parallel"),
        name="diff_proj",
    )(x2, mods3, c_t, sa_t, sb_t, win)


def _post_attn_kernel(o_ref, x_ref, mod_ref, wo_ref, g_ref, b_ref, rw_ref, x1_ref, h2_ref, aff_ref):
    m = mod_ref[0]
    w_hi, w_lo = _split_bf16(rw_ref[...])
    rw = jnp.concatenate([w_hi, w_lo], axis=1)
    for r0 in range(0, o_ref.shape[0], POST_SUB_ROWS):
        rows = slice(r0, r0 + POST_SUB_ROWS)
        t = _dot(o_ref[rows, :], wo_ref[...])
        x1 = _layer_norm(ALPHA * x_ref[rows, :] + (1.0 + m[2:3]) * t, g_ref[...], b_ref[...])
        x1_ref[rows, :] = x1
        h2 = x1 * (1.0 + m[4:5]) + m[3:4]
        h2_ref[rows, :] = h2
        h_hi, h_lo = _split_bf16(h2)
        hw = _dot(h_hi, rw)
        logits = hw[:, :LANES] + (hw[:, LANES:] + _dot(h_lo, w_hi))
        lt = logits.T[:N_EXPERTS]
        mx = jnp.max(lt, axis=0, keepdims=True)
        ex = jnp.exp(lt - mx)
        aff_ref[0, :, rows] = ex / jnp.sum(ex, axis=0, keepdims=True)


def _post_attn(o2, x2, mods3, w_o, ln_g, ln_b, router_w, B, S):
    T, D = x2.shape
    E = N_EXPERTS
    tm = min(POST_TILE, S)
    npb = S // tm
    rw = jnp.pad(router_w, ((0, 0), (0, LANES - E)))
    return pl.pallas_call(
        _post_attn_kernel,
        grid=(T // tm,),
        in_specs=[
            pl.BlockSpec((tm, D), lambda i: (i, 0)),
            pl.BlockSpec((tm, D), lambda i: (i, 0)),
            pl.BlockSpec((1, 6, D), lambda i: (i // npb, 0, 0)),
            pl.BlockSpec((D, D), lambda i: (0, 0)),
            pl.BlockSpec((1, D), lambda i: (0, 0)),
            pl.BlockSpec((1, D), lambda i: (0, 0)),
            pl.BlockSpec((D, LANES), lambda i: (0, 0)),
        ],
        out_specs=[
            pl.BlockSpec((tm, D), lambda i: (i, 0)),
            pl.BlockSpec((tm, D), lambda i: (i, 0)),
            pl.BlockSpec((1, E, tm), lambda i: (i // npb, 0, i % npb)),
        ],
        out_shape=[
            jax.ShapeDtypeStruct((T, D), F32),
            jax.ShapeDtypeStruct((T, D), F32),
            jax.ShapeDtypeStruct((B, E, S), F32),
        ],
        compiler_params=_cparams("parallel"),
        name="post_attn",
    )(o2, x2, mods3, w_o.astype(BF16), ln_g.reshape(1, D), ln_b.reshape(1, D), rw)


def _topk_kernel(aff_ref, idx_ref, gate_ref, pos_ref, off_ref, *, cap):
    a3 = aff_ref[0]
    E, NC, _ = a3.shape
    R = E * NC

    def count3(mask3):
        c = jnp.sum(jnp.where(mask3, 1.0, 0.0), axis=2, keepdims=True)
        return jnp.sum(c, axis=1, keepdims=True)

    def body(i, bits):
        cand = bits | jnp.left_shift(jnp.int32(1), 30 - i)
        reach = count3(a3 >= lax.bitcast_convert_type(cand, F32))
        return jnp.where(reach >= cap, cand, bits)

    thr = lax.bitcast_convert_type(lax.fori_loop(0, 31, body, jnp.zeros((E, 1, 1), jnp.int32)), F32)
    gt3 = a3 > thr
    eq3 = a3 == thr
    need = cap - count3(gt3)

    li = lax.broadcasted_iota(jnp.int32, (LANES, LANES), 0)
    lj = lax.broadcasted_iota(jnp.int32, (LANES, LANES), 1)
    tri_excl = jnp.where(li < lj, 1.0, 0.0).astype(BF16)
    tri_incl = jnp.where(li <= lj, 1.0, 0.0).astype(BF16)
    ri = lax.broadcasted_iota(jnp.int32, (R, R), 0)
    rj = lax.broadcasted_iota(jnp.int32, (R, R), 1)
    same = (ri // NC) == (rj // NC)
    blk_lower = jnp.where(same & (rj < ri), 1.0, 0.0).astype(BF16)

    def chunk_offsets(mask2):
        tot = jnp.sum(mask2, axis=1, keepdims=True)
        totb = jnp.broadcast_to(tot, (R, LANES)).astype(BF16)
        return _dot(blk_lower, totb), tot

    eq2 = jnp.where(eq3, 1.0, 0.0).reshape(R, LANES)
    eq_off, _ = chunk_offsets(eq2)
    eq_rank = (eq_off + _dot(eq2.astype(BF16), tri_excl)).reshape(E, NC, LANES)
    sel3 = gt3 | (eq3 & (eq_rank < need))
    sel2 = jnp.where(sel3, 1.0, 0.0).reshape(R, LANES)
    off2, tot2 = chunk_offsets(sel2)
    cum2 = _dot(sel2.astype(BF16), tri_incl)
    pos_ref[0] = jnp.where(sel3, (off2 + cum2 - 1.0).reshape(E, NC, LANES), -1.0).astype(jnp.int32)
    off_ref[0] = off2.reshape(E, NC, LANES).astype(jnp.int32)
    a2 = a3.reshape(R, LANES)
    a_hi = a2.astype(BF16)
    a_mid = (a2 - a_hi.astype(F32)).astype(BF16)
    a_lo = (a2 - a_hi.astype(F32) - a_mid.astype(F32)).astype(BF16)

    cs = lax.broadcasted_iota(jnp.int32, (NC, cap), 1).astype(F32)
    ks = lax.broadcasted_iota(jnp.int32, (NC, cap), 0).astype(F32)
    ls = lax.broadcasted_iota(jnp.int32, (LANES, cap), 0).astype(F32)
    for e in range(E):
        rows = slice(e * NC, (e + 1) * NC)
        off_e = off2[rows, 0:1]
        end_e = off_e + tot2[rows]
        k_of_c = jnp.sum(jnp.where(end_e <= cs, 1.0, 0.0), axis=0, keepdims=True)
        onehot = ks == k_of_c
        oh = jnp.where(onehot, 1.0, 0.0).astype(BF16)
        off_c = jnp.sum(jnp.where(onehot, off_e, 0.0), axis=0, keepdims=True)
        rank_c = cs[0:1] - off_c
        cum_c = _dot(cum2[rows].T.astype(BF16), oh)
        lane_c = jnp.sum(jnp.where(cum_c <= rank_c, 1.0, 0.0), axis=0, keepdims=True)
        idx_ref[0, e] = (k_of_c * LANES + lane_c).astype(jnp.int32)
        g_c = (_dot(a_hi[rows].T, oh) + _dot(a_mid[rows].T, oh)) + _dot(a_lo[rows].T, oh)
        gate_ref[0, e] = jnp.sum(jnp.where(ls == lane_c, g_c, 0.0), axis=0, keepdims=True)


def _topk(aff, cap):
    B, E, S = aff.shape
    NC = S // LANES
    slot_spec = pl.BlockSpec((1, E, 1, cap), lambda b: (b, 0, 0, 0))
    tok_spec = pl.BlockSpec((1, E, NC, LANES), lambda b: (b, 0, 0, 0))
    return pl.pallas_call(
        functools.partial(_topk_kernel, cap=cap),
        grid=(B,),
        in_specs=[tok_spec],
        out_specs=[slot_spec, slot_spec, tok_spec, tok_spec],
        out_shape=[
            jax.ShapeDtypeStruct((B, E, 1, cap), jnp.int32),
            jax.ShapeDtypeStruct((B, E, 1, cap), F32),
            jax.ShapeDtypeStruct((B, E, NC, LANES), jnp.int32),
            jax.ShapeDtypeStruct((B, E, NC, LANES), jnp.int32),
        ],
        compiler_params=_cparams("parallel"),
        name="ec_topk",
    )(aff.reshape(B, E, NC, LANES))


def _dispatch_kernel(idx_ref, h_ref, xe_ref):
    cap = xe_ref.shape[1]

    def group(g, carry):
        rows = [h_ref[pl.ds(idx_ref[0, 0, g * SLOT_ALIGN + r], 1), :] for r in range(SLOT_ALIGN)]
        start = pl.multiple_of(g * SLOT_ALIGN, SLOT_ALIGN)
        xe_ref[0, pl.ds(start, SLOT_ALIGN), :] = jnp.concatenate(rows, axis=0).astype(BF16)
        return carry

    lax.fori_loop(0, cap // SLOT_ALIGN, group, 0)


def _dispatch(idx, h2, B, S, cap):
    E = idx.shape[1]
    D = h2.shape[1]
    return pl.pallas_call(
        _dispatch_kernel,
        grid=(B, E),
        in_specs=[
            pl.BlockSpec((1, 1, cap), lambda b, e: (b * E + e, 0, 0), memory_space=pltpu.SMEM),
            pl.BlockSpec((S, D), lambda b, e: (b, 0)),
        ],
        out_specs=pl.BlockSpec((1, cap, D), lambda b, e: (e, b, 0)),
        out_shape=jax.ShapeDtypeStruct((E, B * cap, D), BF16),
        compiler_params=_cparams("parallel", "parallel"),
        name="moe_dispatch",
    )(idx.reshape(B * E, 1, cap), h2)


def _moe_ffn_kernel(xe_ref, gate_ref, wg_ref, wu_ref, wo_ref, y_ref, acc_ref):
    f = pl.program_id(1)

    @pl.when(f == 0)
    def _():
        acc_ref[...] = jnp.zeros_like(acc_ref)

    xe = xe_ref[0]
    g = _dot(xe, wg_ref[0, 0].astype(BF16))
    u = _dot(xe, wu_ref[0, 0].astype(BF16))
    act = (g * (1.0 / (1.0 + jnp.exp(-g))) * u).astype(BF16)
    acc_ref[...] += _dot(act, wo_ref[0, 0].astype(BF16))

    @pl.when(f == pl.num_programs(1) - 1)
    def _():
        diag = (lax.broadcasted_iota(jnp.int32, (LANES, LANES), 0)
                == lax.broadcasted_iota(jnp.int32, (LANES, LANES), 1))
        for r0 in range(0, acc_ref.shape[0], LANES):
            col = jnp.sum(jnp.where(diag, gate_ref[0, :, r0:r0 + LANES], 0.0), axis=1, keepdims=True)
            y_ref[0, r0:r0 + LANES, :] = (acc_ref[r0:r0 + LANES, :] * col).astype(BF16)


def _moe_ffn(xe, gate_row, w_in, w_out, layer):
    E, M, D = xe.shape
    FF = w_out.shape[2]
    tf = FF_TILE
    nf = FF // tf
    return pl.pallas_call(
        _moe_ffn_kernel,
        grid=(E, nf),
        in_specs=[
            pl.BlockSpec((1, M, D), lambda e, f: (e, 0, 0)),
            pl.BlockSpec((1, 1, M), lambda e, f: (e, 0, 0)),
            pl.BlockSpec((1, 1, D, tf), lambda e, f: (layer, e, 0, f)),
            pl.BlockSpec((1, 1, D, tf), lambda e, f: (layer, e, 0, nf + f)),
            pl.BlockSpec((1, 1, tf, D), lambda e, f: (layer, e, f, 0)),
        ],
        out_specs=pl.BlockSpec((1, M, D), lambda e, f: (e, 0, 0)),
        out_shape=jax.ShapeDtypeStruct((E, M, D), BF16),
        scratch_shapes=[pltpu.VMEM((M, D), F32)],
        compiler_params=_cparams("parallel", "arbitrary"),
        name="moe_ffn",
    )(xe, gate_row, w_in, w_in, w_out)


def _combine_ln_kernel(win_ref, pos_ref, y_ref, x_ref, mod_ref, g_ref, b_ref, o_ref, *, window):
    b = pl.program_id(0)
    j = pl.program_id(1)
    n_exp = y_ref.shape[0]
    tiles = pos_ref.shape[1] // LANES
    slot = lax.broadcasted_iota(jnp.int32, (LANES, window), 1)
    m = mod_ref[0]
    for t in range(tiles):
        rows = slice(t * LANES, (t + 1) * LANES)
        pos = pos_ref[0, rows, :]
        tile = (b * pl.num_programs(1) + j) * tiles + t
        f = None
        for e in range(n_exp):
            start = pl.multiple_of(win_ref[tile * n_exp + e], SLOT_ALIGN)
            onehot = jnp.where(pos[:, e:e + 1] - start == slot, 1.0, 0.0).astype(BF16)
            part = _dot(onehot, y_ref[e, pl.ds(start, window), :])
            f = part if f is None else f + part
        o_ref[rows, :] = _layer_norm(ALPHA * x_ref[rows, :] + (1.0 + m[5:6]) * f, g_ref[...], b_ref[...])


def _combine_ln(pos_t, win, y, x1, mods3, ln_g, ln_b, B, S, cap, window):
    E, _, D = y.shape
    tm = min(COMBINE_TILE, S)
    nj = S // tm
    row = pl.BlockSpec((tm, D), lambda b, j, w: (b * nj + j, 0))
    vec = pl.BlockSpec((1, D), lambda b, j, w: (0, 0))
    return pl.pallas_call(
        functools.partial(_combine_ln_kernel, window=window),
        grid_spec=pltpu.PrefetchScalarGridSpec(
            num_scalar_prefetch=1,
            grid=(B, nj),
            in_specs=[
                pl.BlockSpec((1, tm, E), lambda b, j, w: (b, j, 0)),
                pl.BlockSpec((E, cap, D), lambda b, j, w: (0, b, 0)),
                row,
                pl.BlockSpec((1, 6, D), lambda b, j, w: (b, 0, 0)),
                vec,
                vec,
            ],
            out_specs=row,
        ),
        out_shape=jax.ShapeDtypeStruct((B * S, D), F32),
        compiler_params=_cparams("parallel", "arbitrary"),
        name="moe_combine_ln",
    )(win, pos_t, y, x1, mods3, ln_g.reshape(1, D), ln_b.reshape(1, D))


def _rope_tables(positions):
    B, S = positions.shape
    pos = positions.astype(F32).reshape(B * S, 1)
    inv_m = MLA_THETA ** (-jnp.arange(0, MLA_ROPE, 2, dtype=F32) / MLA_ROPE)
    ang_m = pos * jnp.tile(inv_m, LANES // (MLA_ROPE // 2))
    inv_d = ROPE_THETA ** (-jnp.arange(0, DIFF_ROT, 2, dtype=F32) / DIFF_ROT)
    hr = DIFF_ROT // 2
    rest = jnp.zeros((DIFF_HEAD_DIM - DIFF_ROT,), F32)
    ang_d = pos * jnp.tile(jnp.concatenate([inv_d, inv_d, rest]), LANES // DIFF_HEAD_DIM)
    lane = jnp.arange(LANES) % DIFF_HEAD_DIM
    sin_d = jnp.sin(ang_d)
    sa_d = jnp.where(lane < hr, -sin_d, 0.0)
    sb_d = jnp.where((lane >= hr) & (lane < DIFF_ROT), sin_d, 0.0)
    return jnp.cos(ang_m), jnp.sin(ang_m), jnp.cos(ang_d), sa_d, sb_d


def _moe_block(x1, h2, aff, mods3, w_in, w_out, layer, ln_g, ln_b, B, S):
    E = N_EXPERTS
    cap = max(1, EC_CAPACITY_FACTOR * S // E)
    idx, gate, pos, off = _topk(aff, cap)
    gate_row = gate[:, :, 0, :].transpose(1, 0, 2).reshape(E, 1, B * cap)
    xe = _dispatch(idx, h2, B, S, cap)
    y = _moe_ffn(xe, gate_row, w_in, w_out, layer)
    window = min(COMBINE_WINDOW, cap)
    pos_t = pos.reshape(B, E, S).transpose(0, 2, 1)
    win = jnp.minimum(off[:, :, :, 0] // SLOT_ALIGN * SLOT_ALIGN, cap - window)
    win = win.transpose(0, 2, 1).reshape(-1)
    return _combine_ln(pos_t, win, y, x1, mods3, ln_g, ln_b, B, S, cap, window)


def kernel(x, c, positions, ada_w, ada_b, ln1_g, ln1_b, ln2_g, ln2_b, mla_w_in, mla_q_norm, mla_kv_norm,
           mla_w_uq, mla_w_ukv, mla_w_o, diff_w_in, diff_lambda, diff_subln, diff_w_o, router_w, moe_w_in,
           moe_w_out):
    B, S, D = x.shape
    cos_m, sin_m, c_d, sa_d, sb_d = _rope_tables(positions)
    mods = _mods(c, ada_w, ada_b)
    x2 = x.reshape(B * S, D)
    for i in range(DEPTH):
        mods3 = mods[i, :B].reshape(B, 6, D)
        j = i // 2
        if i % 2 == 0:
            q, k, v = _mla_proj(x2, mods3, cos_m, sin_m, mla_w_in[j], mla_q_norm[j], mla_kv_norm[j],
                                mla_w_uq[j], mla_w_ukv[j], B, S)
            o = _attention(_mla_attn_kernel, [], 1, q, k, v, "mla_attn")
            w_o = mla_w_o[j]
        else:
            lambda_init = 0.8 - 0.6 * math.exp(-0.3 * i)
            q, k, v = _diff_proj(x2, mods3, c_d, sa_d, sb_d, diff_w_in[j], B, S)
            o = _attention(functools.partial(_diff_attn_kernel, lambda_init=lambda_init),
                           [diff_lambda[j], diff_subln[j].reshape(DIFF_V, 1)], 2, q, k, v, "diff_attn")
            w_o = diff_w_o[j]
        x1, h2, aff = _post_attn(o.reshape(B * S, D), x2, mods3, w_o, ln1_g[i], ln1_b[i], router_w[i], B, S)
        x2 = _moe_block(x1, h2, aff, mods3, moe_w_in, moe_w_out, i, ln2_g[i], ln2_b[i], B, S)
    return x2.reshape(B, S, D)
```

```python
import functools
import math

import jax
import jax.numpy as jnp
from jax import lax
from jax.experimental import pallas as pl
from jax.experimental.pallas import tpu as pltpu

F32 = jnp.float32
BF16 = jnp.bfloat16

D_MODEL = 1024
DEPTH = 2
MLA_HEADS = 8
MLA_Q_LORA = 384
MLA_KV_LORA = 256
MLA_NOPE = 128
MLA_ROPE = 64
MLA_V = 128
MLA_THETA = 10000.0
DIFF_HEADS = 8
DIFF_HEAD_DIM = 64
DIFF_V = 2 * DIFF_HEAD_DIM
DIFF_ROT = DIFF_HEAD_DIM // 4
ROPE_THETA = 500000.0
N_EXPERTS = 16
EXPERT_FF = 2048
EC_CAPACITY_FACTOR = 2
NORM_EPS = 1e-5
LATENT_EPS = 1e-6
ALPHA = (2 * DEPTH) ** 0.25

LANES = 128
VMEM_LIMIT = 56 * 1024 * 1024

TOKEN_TILE = 512
PROJ_SUB_ROWS = 256
POST_TILE = 1024
POST_SUB_ROWS = 256
KEY_CHUNK = 512
ROW_BLOCK = 256
LOG2E = math.log2(math.e)
FF_TILE = 512
SLOT_ALIGN = 16
COMBINE_WINDOW = 256
COMBINE_TILE = 512


def _cparams(*sem):
    return pltpu.CompilerParams(dimension_semantics=sem, vmem_limit_bytes=VMEM_LIMIT)


def _dot(a, b):
    return jnp.dot(a, b, preferred_element_type=F32)


def _split_bf16(a):
    hi = a.astype(BF16)
    lo = (a - hi.astype(F32)).astype(BF16)
    return hi, lo


def _dot3(a, b):
    ah, al = _split_bf16(a)
    bh, bl = _split_bf16(b)
    return _dot(ah, bh) + (_dot(ah, bl) + _dot(al, bh))


def _layer_norm(y, g, b):
    mu = jnp.mean(y, axis=-1, keepdims=True)
    yc = y - mu
    var = jnp.mean(yc * yc, axis=-1, keepdims=True)
    return yc * lax.rsqrt(var + NORM_EPS) * g + b


def _mods_kernel(c_ref, w_ref, b_ref, o_ref):
    c = c_ref[...]
    a = c * (1.0 / (1.0 + jnp.exp(-c)))
    o_ref[0] = _dot3(a, w_ref[0]) + b_ref[0]


def _mods(c, ada_w, ada_b):
    B, D = c.shape
    L, _, N = ada_w.shape
    rows = 8
    cp = jnp.zeros((rows, D), F32).at[:B].set(c)
    tn = 1536
    return pl.pallas_call(
        _mods_kernel,
        grid=(L, N // tn),
        in_specs=[
            pl.BlockSpec((rows, D), lambda l, j: (0, 0)),
            pl.BlockSpec((1, D, tn), lambda l, j: (l, 0, j)),
            pl.BlockSpec((1, 1, tn), lambda l, j: (l, 0, j)),
        ],
        out_specs=pl.BlockSpec((1, rows, tn), lambda l, j: (l, 0, j)),
        out_shape=jax.ShapeDtypeStruct((L, rows, N), F32),
        compiler_params=_cparams("parallel", "parallel"),
        name="adaln_mods",
    )(cp, ada_w, ada_b.reshape(L, 1, N))


def _mla_proj_kernel(x_ref, mod_ref, cos_ref, sin_ref, win_ref, qn_ref, kvn_ref, wq_ref, wkv_ref,
                     q_ref, k_ref, v_ref, *, scale):
    m = mod_ref[0]
    half = MLA_ROPE // 2
    hw = MLA_HEADS * LANES
    for r0 in range(0, x_ref.shape[0], PROJ_SUB_ROWS):
        rows = slice(r0, r0 + PROJ_SUB_ROWS)
        h = (x_ref[rows, :] * (1.0 + m[1:2]) + m[0:1]).astype(BF16)
        z = _dot(h, win_ref[...])
        cq = z[:, :MLA_Q_LORA]
        ckv = z[:, MLA_Q_LORA:MLA_Q_LORA + MLA_KV_LORA]
        kr = z[:, MLA_Q_LORA + MLA_KV_LORA:]
        lane = lax.broadcasted_iota(jnp.int32, kr.shape, 1)
        cos = cos_ref[rows, :]
        sin = sin_ref[rows, :]
        sin_up = jnp.where(lane < half, -sin, 0.0)
        sin_dn = jnp.where((lane >= half) & (lane < MLA_ROPE), sin, 0.0)

        def rope(t):
            return t * cos + pltpu.roll(t, LANES - half, 1) * sin_up + pltpu.roll(t, half, 1) * sin_dn

        qn = (cq * lax.rsqrt(jnp.mean(cq * cq, axis=-1, keepdims=True) + LATENT_EPS) * qn_ref[...]).astype(BF16)
        kvn = (ckv * lax.rsqrt(jnp.mean(ckv * ckv, axis=-1, keepdims=True) + LATENT_EPS)
               * kvn_ref[...]).astype(BF16)
        q = _dot(qn, wq_ref[...])
        kv = _dot(kvn, wkv_ref[...])
        kpe = rope(kr).astype(BF16)
        for hh in range(MLA_HEADS):
            a, b = hh * LANES, (hh + 1) * LANES
            q_ref[0, hh, rows, 0:LANES] = (q[:, a:b] * scale).astype(BF16)
            q_ref[0, hh, rows, LANES:2 * LANES] = (rope(q[:, hw + a:hw + b]) * scale).astype(BF16)
            k_ref[0, hh, rows, 0:LANES] = kv[:, a:b].astype(BF16)
            k_ref[0, hh, rows, LANES:2 * LANES] = kpe
            v_ref[0, hh, rows, :] = kv[:, hw + a:hw + b].astype(BF16)


def _mla_proj(x2, mods3, cos_t, sin_t, w_in, q_norm, kv_norm, w_uq, w_ukv, B, S):
    T, D = x2.shape
    H = MLA_HEADS
    tm = min(TOKEN_TILE, S)
    npb = S // tm
    win_p = jnp.pad(w_in, ((0, 0), (0, LANES - MLA_ROPE))).astype(BF16)
    wq3 = w_uq.reshape(MLA_Q_LORA, H, MLA_NOPE + MLA_ROPE)
    rope_p = jnp.pad(wq3[:, :, MLA_NOPE:], ((0, 0), (0, 0), (0, LANES - MLA_ROPE))).reshape(MLA_Q_LORA, H * LANES)
    wq_p = jnp.concatenate([wq3[:, :, :MLA_NOPE].reshape(MLA_Q_LORA, H * MLA_NOPE), rope_p],
                           axis=1).astype(BF16)
    wkv3 = w_ukv.reshape(MLA_KV_LORA, H, MLA_NOPE + MLA_V)
    wkv_p = jnp.concatenate([wkv3[:, :, :MLA_NOPE].reshape(MLA_KV_LORA, H * MLA_NOPE),
                             wkv3[:, :, MLA_NOPE:].reshape(MLA_KV_LORA, H * MLA_V)], axis=1).astype(BF16)
    scale = (MLA_NOPE + MLA_ROPE) ** -0.5 * LOG2E
    full = lambda shape: pl.BlockSpec(shape, lambda i: (0,) * len(shape))
    return pl.pallas_call(
        functools.partial(_mla_proj_kernel, scale=scale),
        grid=(T // tm,),
        in_specs=[
            pl.BlockSpec((tm, D), lambda i: (i, 0)),
            pl.BlockSpec((1, 6, D), lambda i: (i // npb, 0, 0)),
            pl.BlockSpec((tm, LANES), lambda i: (i, 0)),
            pl.BlockSpec((tm, LANES), lambda i: (i, 0)),
            full(win_p.shape),
            full((1, MLA_Q_LORA)),
            full((1, MLA_KV_LORA)),
            full(wq_p.shape),
            full(wkv_p.shape),
        ],
        out_specs=[
            pl.BlockSpec((1, H, tm, 2 * LANES), lambda i: (i // npb, 0, i % npb, 0)),
            pl.BlockSpec((1, H, tm, 2 * LANES), lambda i: (i // npb, 0, i % npb, 0)),
            pl.BlockSpec((1, H, tm, MLA_V), lambda i: (i // npb, 0, i % npb, 0)),
        ],
        out_shape=[
            jax.ShapeDtypeStruct((B, H, S, 2 * LANES), BF16),
            jax.ShapeDtypeStruct((B, H, S, 2 * LANES), BF16),
            jax.ShapeDtypeStruct((B, H, S, MLA_V), BF16),
        ],
        compiler_params=_cparams("parallel"),
        name="mla_proj",
    )(x2, mods3, cos_t, sin_t, win_p, q_norm.reshape(1, -1), kv_norm.reshape(1, -1), wq_p, wkv_p)


_NT = (((1,), (1,)), ((), ()))


def _attn_pipeline(load_q, n_maps, k_ref, vt_ref, st_ref, m_ref, emit, n_blocks):
    S = k_ref.shape[2]
    tk = min(KEY_CHUNK, S)
    chunks = [slice(c * tk, (c + 1) * tk) for c in range(S // tk)]

    def scores(blk, slot):
        for mp, qb in enumerate(load_q(blk)):
            m = None
            for ck in chunks:
                sj = lax.dot_general(k_ref[0, 0, ck, :], qb, _NT, preferred_element_type=F32)
                st_ref[slot, mp, ck, :] = sj
                mj = jnp.max(sj, axis=0, keepdims=True)
                m = mj if m is None else jnp.maximum(m, mj)
            m_ref[slot, mp] = m

    def attend(blk, slot):
        res = []
        for mp in range(n_maps):
            m = m_ref[slot, mp]
            l = None
            acc = None
            for ck in chunks:
                e = jnp.exp2(st_ref[slot, mp, ck, :] - m)
                lj = jnp.sum(e, axis=0, keepdims=True)
                l = lj if l is None else l + lj
                pv = _dot(vt_ref[:, ck], e.astype(BF16))
                acc = pv if acc is None else acc + pv
            res.append((acc, l))
        emit(blk, res)

    scores(0, 0)

    def body(jj, carry):
        j = 2 * jj
        for u in range(2):
            scores(jnp.minimum(j + u + 1, n_blocks - 1), (u + 1) % 2)
            attend(j + u, u % 2)
        return carry

    lax.fori_loop(0, n_blocks // 2, body, 0)


def _block_rows(blk):
    return pl.ds(pl.multiple_of(blk * ROW_BLOCK, ROW_BLOCK), ROW_BLOCK)


def _mla_attn_kernel(q_ref, k_ref, v_ref, o_ref, vt_ref, st_ref, m_ref):
    vt_ref[...] = v_ref[0, 0].astype(F32).T.astype(BF16)

    def emit(blk, res):
        (acc, l), = res
        o_ref[0, _block_rows(blk), :] = (acc * (1.0 / l)).T.astype(BF16)

    _attn_pipeline(lambda blk: [q_ref[0, 0, _block_rows(blk), :]], 1, k_ref, vt_ref, st_ref, m_ref, emit,
                   q_ref.shape[2] // ROW_BLOCK)


def _diff_attn_kernel(lam_ref, subln_ref, q_ref, k_ref, v_ref, o_ref, vt_ref, st_ref, m_ref, *, lambda_init):
    vt_ref[...] = v_ref[0, 0].astype(F32).T.astype(BF16)
    lf = lam_ref[...]
    lam = (jnp.exp(jnp.sum(lf[0:1] * lf[1:2], axis=-1, keepdims=True))
           - jnp.exp(jnp.sum(lf[2:3] * lf[3:4], axis=-1, keepdims=True)) + lambda_init)

    def load_q(blk):
        q = q_ref[0, 0, _block_rows(blk), :]
        lane = lax.broadcasted_iota(jnp.int32, q.shape, 1)
        zero = jnp.zeros_like(q)
        return [jnp.where(lane < DIFF_HEAD_DIM, q, zero), jnp.where(lane >= DIFF_HEAD_DIM, q, zero)]

    def emit(blk, res):
        (acc1, l1), (acc2, l2) = res
        o = acc1 * (1.0 / l1) - acc2 * (lam / l2)
        o = o * lax.rsqrt(jnp.mean(o * o, axis=0, keepdims=True) + NORM_EPS) * subln_ref[...]
        o_ref[0, _block_rows(blk), :] = (o * (1.0 - lambda_init)).T.astype(BF16)

    _attn_pipeline(load_q, 2, k_ref, vt_ref, st_ref, m_ref, emit, q_ref.shape[2] // ROW_BLOCK)


def _attention(kernel_fn, extra, n_maps, q, k, v, name):
    B, H, S, dq = q.shape
    dv = v.shape[-1]
    assert (S // ROW_BLOCK) % 2 == 0
    extra_specs = [pl.BlockSpec(a.shape, lambda b, h, n=a.ndim: (0,) * n) for a in extra]
    return pl.pallas_call(
        kernel_fn,
        grid=(B, H),
        in_specs=extra_specs + [
            pl.BlockSpec((1, 1, S, dq), lambda b, h: (b, h, 0, 0)),
            pl.BlockSpec((1, 1, S, dq), lambda b, h: (b, h, 0, 0)),
            pl.BlockSpec((1, 1, S, dv), lambda b, h: (b, h, 0, 0)),
        ],
        out_specs=pl.BlockSpec((1, S, dv), lambda b, h: (b, 0, h)),
        out_shape=jax.ShapeDtypeStruct((B, S, H * dv), BF16),
        scratch_shapes=[
            pltpu.VMEM((dv, S), BF16),
            pltpu.VMEM((2, n_maps, S, ROW_BLOCK), F32),
            pltpu.VMEM((2, n_maps, 1, ROW_BLOCK), F32),
        ],
        compiler_params=_cparams("parallel", "parallel"),
        name=name,
    )(*extra, q, k, v)


def _diff_proj_kernel(x_ref, mod_ref, cos_ref, sin_ref, win_ref, q_ref, k_ref, v_ref, *, scale):
    m = mod_ref[0]
    h = (x_ref[...] * (1.0 + m[1:2]) + m[0:1]).astype(BF16)
    z = _dot(h, win_ref[...])
    cos_t = cos_ref[...]
    sin_t = sin_ref[...]
    lane = lax.broadcasted_iota(jnp.int32, cos_t.shape, 1)
    low = lane < DIFF_HEAD_DIM
    r = lane % DIFF_HEAD_DIM
    hr = DIFF_ROT // 2
    c = jnp.where(r < DIFF_ROT, jnp.where(low, pltpu.roll(cos_t, DIFF_HEAD_DIM, 1), cos_t), 1.0)
    s = jnp.where(low, pltpu.roll(sin_t, DIFF_HEAD_DIM, 1), sin_t)
    sa = jnp.where(r < hr, -s, 0.0)
    sb = jnp.where((r >= hr) & (r < DIFF_ROT), s, 0.0)
    hw = DIFF_HEADS * LANES

    def rope(t):
        return t * c + pltpu.roll(t, LANES - DIFF_ROT // 2, 1) * sa + pltpu.roll(t, DIFF_ROT // 2, 1) * sb

    for hh in range(DIFF_HEADS):
        a, b = hh * LANES, (hh + 1) * LANES
        q_ref[0, hh] = (rope(z[:, a:b]) * scale).astype(BF16)
        k_ref[0, hh] = rope(z[:, hw + a:hw + b]).astype(BF16)
        v_ref[0, hh] = z[:, 2 * hw + a:2 * hw + b].astype(BF16)


def _diff_proj(x2, mods3, cos_t, sin_t, w_in, B, S):
    T, D = x2.shape
    H = DIFF_HEADS
    tm = min(TOKEN_TILE, S)
    npb = S // tm
    win = w_in.astype(BF16)
    hd = jax.ShapeDtypeStruct((B, H, S, LANES), BF16)
    hspec = pl.BlockSpec((1, H, tm, LANES), lambda i: (i // npb, 0, i % npb, 0))
    return pl.pallas_call(
        functools.partial(_diff_proj_kernel, scale=DIFF_HEAD_DIM ** -0.5 * LOG2E),
        grid=(T // tm,),
        in_specs=[
            pl.BlockSpec((tm, D), lambda i: (i, 0)),
            pl.BlockSpec((1, 6, D), lambda i: (i // npb, 0, 0)),
            pl.BlockSpec((tm, LANES), lambda i: (i, 0)),
            pl.BlockSpec((tm, LANES), lambda i: (i, 0)),
            pl.BlockSpec(win.shape, lambda i: (0, 0)),
        ],
        out_specs=[hspec, hspec, hspec],
        out_shape=[hd, hd, hd],
        compiler_params=_cparams("parallel"),
        name="diff_proj",
    )(x2, mods3, cos_t, sin_t, win)


def _post_attn_kernel(o_ref, x_ref, mod_ref, wo_ref, g_ref, b_ref, rw_ref, x1_ref, h2_ref, aff_ref):
    m = mod_ref[0]
    w_hi, w_lo = _split_bf16(rw_ref[...])
    rw = jnp.concatenate([w_hi, w_lo], axis=1)
    for r0 in range(0, o_ref.shape[0], POST_SUB_ROWS):
        rows = slice(r0, r0 + POST_SUB_ROWS)
        t = _dot(o_ref[rows, :], wo_ref[...])
        x1 = _layer_norm(ALPHA * x_ref[rows, :] + (1.0 + m[2:3]) * t, g_ref[...], b_ref[...])
        x1_ref[rows, :] = x1
        h2 = x1 * (1.0 + m[4:5]) + m[3:4]
        h2_ref[rows, :] = h2
        h_hi, h_lo = _split_bf16(h2)
        hw = _dot(h_hi, rw)
        logits = hw[:, :LANES] + (hw[:, LANES:] + _dot(h_lo, w_hi))
        lt = logits.T[:N_EXPERTS]
        mx = jnp.max(lt, axis=0, keepdims=True)
        ex = jnp.exp(lt - mx)
        aff_ref[0, :, rows] = ex / jnp.sum(ex, axis=0, keepdims=True)


def _post_attn(o2, x2, mods3, w_o, ln_g, ln_b, router_w, B, S):
    T, D = x2.shape
    E = N_EXPERTS
    tm = min(POST_TILE, S)
    npb = S // tm
    rw = jnp.pad(router_w, ((0, 0), (0, LANES - E)))
    return pl.pallas_call(
        _post_attn_kernel,
        grid=(T // tm,),
        in_specs=[
            pl.BlockSpec((tm, D), lambda i: (i, 0)),
            pl.BlockSpec((tm, D), lambda i: (i, 0)),
            pl.BlockSpec((1, 6, D), lambda i: (i // npb, 0, 0)),
            pl.BlockSpec((D, D), lambda i: (0, 0)),
            pl.BlockSpec((1, D), lambda i: (0, 0)),
            pl.BlockSpec((1, D), lambda i: (0, 0)),
            pl.BlockSpec((D, LANES), lambda i: (0, 0)),
        ],
        out_specs=[
            pl.BlockSpec((tm, D), lambda i: (i, 0)),
            pl.BlockSpec((tm, D), lambda i: (i, 0)),
            pl.BlockSpec((1, E, tm), lambda i: (i // npb, 0, i % npb)),
        ],
        out_shape=[
            jax.ShapeDtypeStruct((T, D), F32),
            jax.ShapeDtypeStruct((T, D), F32),
            jax.ShapeDtypeStruct((B, E, S), F32),
        ],
        compiler_params=_cparams("parallel"),
        name="post_attn",
    )(o2, x2, mods3, w_o.astype(BF16), ln_g.reshape(1, D), ln_b.reshape(1, D), rw)


def _topk_kernel(aff_ref, idx_ref, gate_ref, pos_ref, off_ref, *, cap):
    a3 = aff_ref[0]
    E, NC, _ = a3.shape
    R = E * NC

    def count3(mask3):
        c = jnp.sum(jnp.where(mask3, 1.0, 0.0), axis=2, keepdims=True)
        return jnp.sum(c, axis=1, keepdims=True)

    def body(i, bits):
        cand = bits | jnp.left_shift(jnp.int32(1), 30 - i)
        reach = count3(a3 >= lax.bitcast_convert_type(cand, F32))
        return jnp.where(reach >= cap, cand, bits)

    thr = lax.bitcast_convert_type(lax.fori_loop(0, 31, body, jnp.zeros((E, 1, 1), jnp.int32)), F32)
    gt3 = a3 > thr
    eq3 = a3 == thr
    need = cap - count3(gt3)

    li = lax.broadcasted_iota(jnp.int32, (LANES, LANES), 0)
    lj = lax.broadcasted_iota(jnp.int32, (LANES, LANES), 1)
    tri_excl = jnp.where(li < lj, 1.0, 0.0).astype(BF16)
    tri_incl = jnp.where(li <= lj, 1.0, 0.0).astype(BF16)
    ri = lax.broadcasted_iota(jnp.int32, (R, R), 0)
    rj = lax.broadcasted_iota(jnp.int32, (R, R), 1)
    same = (ri // NC) == (rj // NC)
    blk_lower = jnp.where(same & (rj < ri), 1.0, 0.0).astype(BF16)

    def chunk_offsets(mask2):
        tot = jnp.sum(mask2, axis=1, keepdims=True)
        totb = jnp.broadcast_to(tot, (R, LANES)).astype(BF16)
        return _dot(blk_lower, totb), tot

    eq2 = jnp.where(eq3, 1.0, 0.0).reshape(R, LANES)
    eq_off, _ = chunk_offsets(eq2)
    eq_rank = (eq_off + _dot(eq2.astype(BF16), tri_excl)).reshape(E, NC, LANES)
    sel3 = gt3 | (eq3 & (eq_rank < need))
    sel2 = jnp.where(sel3, 1.0, 0.0).reshape(R, LANES)
    off2, tot2 = chunk_offsets(sel2)
    cum2 = _dot(sel2.astype(BF16), tri_incl)
    pos_ref[0] = jnp.where(sel3, (off2 + cum2 - 1.0).reshape(E, NC, LANES), -1.0).astype(jnp.int32)
    off_ref[0] = off2.reshape(E, NC, LANES).astype(jnp.int32)
    a2 = a3.reshape(R, LANES)
    a_hi = a2.astype(BF16)
    a_mid = (a2 - a_hi.astype(F32)).astype(BF16)
    a_lo = (a2 - a_hi.astype(F32) - a_mid.astype(F32)).astype(BF16)

    cs = lax.broadcasted_iota(jnp.int32, (NC, cap), 1).astype(F32)
    ks = lax.broadcasted_iota(jnp.int32, (NC, cap), 0).astype(F32)
    ls = lax.broadcasted_iota(jnp.int32, (LANES, cap), 0).astype(F32)
    for e in range(E):
        rows = slice(e * NC, (e + 1) * NC)
        off_e = off2[rows, 0:1]
        end_e = off_e + tot2[rows]
        k_of_c = jnp.sum(jnp.where(end_e <= cs, 1.0, 0.0), axis=0, keepdims=True)
        onehot = ks == k_of_c
        oh = jnp.where(onehot, 1.0, 0.0).astype(BF16)
        off_c = jnp.sum(jnp.where(onehot, off_e, 0.0), axis=0, keepdims=True)
        rank_c = cs[0:1] - off_c
        cum_c = _dot(cum2[rows].T.astype(BF16), oh)
        lane_c = jnp.sum(jnp.where(cum_c <= rank_c, 1.0, 0.0), axis=0, keepdims=True)
        idx_ref[0, e] = (k_of_c * LANES + lane_c).astype(jnp.int32)
        g_c = (_dot(a_hi[rows].T, oh) + _dot(a_mid[rows].T, oh)) + _dot(a_lo[rows].T, oh)
        gate_ref[0, e] = jnp.sum(jnp.where(ls == lane_c, g_c, 0.0), axis=0, keepdims=True)


def _topk(aff, cap):
    B, E, S = aff.shape
    NC = S // LANES
    slot_spec = pl.BlockSpec((1, E, 1, cap), lambda b: (b, 0, 0, 0))
    tok_spec = pl.BlockSpec((1, E, NC, LANES), lambda b: (b, 0, 0, 0))
    return pl.pallas_call(
        functools.partial(_topk_kernel, cap=cap),
        grid=(B,),
        in_specs=[tok_spec],
        out_specs=[slot_spec, slot_spec, tok_spec, tok_spec],
        out_shape=[
            jax.ShapeDtypeStruct((B, E, 1, cap), jnp.int32),
            jax.ShapeDtypeStruct((B, E, 1, cap), F32),
            jax.ShapeDtypeStruct((B, E, NC, LANES), jnp.int32),
            jax.ShapeDtypeStruct((B, E, NC, LANES), jnp.int32),
        ],
        compiler_params=_cparams("parallel"),
        name="ec_topk",
    )(aff.reshape(B, E, NC, LANES))


def _dispatch_kernel(idx_ref, h_ref, xe_ref):
    cap = xe_ref.shape[1]

    def group(g, carry):
        rows = [h_ref[pl.ds(idx_ref[0, 0, g * SLOT_ALIGN + r], 1), :] for r in range(SLOT_ALIGN)]
        start = pl.multiple_of(g * SLOT_ALIGN, SLOT_ALIGN)
        xe_ref[0, pl.ds(start, SLOT_ALIGN), :] = jnp.concatenate(rows, axis=0).astype(BF16)
        return carry

    lax.fori_loop(0, cap // SLOT_ALIGN, group, 0)


def _dispatch(idx, h2, B, S, cap):
    E = idx.shape[1]
    D = h2.shape[1]
    return pl.pallas_call(
        _dispatch_kernel,
        grid=(B, E),
        in_specs=[
            pl.BlockSpec((1, 1, cap), lambda b, e: (b * E + e, 0, 0), memory_space=pltpu.SMEM),
            pl.BlockSpec((S, D), lambda b, e: (b, 0)),
        ],
        out_specs=pl.BlockSpec((1, cap, D), lambda b, e: (e, b, 0)),
        out_shape=jax.ShapeDtypeStruct((E, B * cap, D), BF16),
        compiler_params=_cparams("parallel", "parallel"),
        name="moe_dispatch",
    )(idx.reshape(B * E, 1, cap), h2)


def _moe_ffn_kernel(xe_ref, gate_ref, wg_ref, wu_ref, wo_ref, y_ref, acc_ref):
    f = pl.program_id(1)

    @pl.when(f == 0)
    def _():
        acc_ref[...] = jnp.zeros_like(acc_ref)

    xe = xe_ref[0]
    g = _dot(xe, wg_ref[0, 0].astype(BF16))
    u = _dot(xe, wu_ref[0, 0].astype(BF16))
    act = (g * (1.0 / (1.0 + jnp.exp(-g))) * u).astype(BF16)
    acc_ref[...] += _dot(act, wo_ref[0, 0].astype(BF16))

    @pl.when(f == pl.num_programs(1) - 1)
    def _():
        diag = (lax.broadcasted_iota(jnp.int32, (LANES, LANES), 0)
                == lax.broadcasted_iota(jnp.int32, (LANES, LANES), 1))
        for r0 in range(0, acc_ref.shape[0], LANES):
            col = jnp.sum(jnp.where(diag, gate_ref[0, :, r0:r0 + LANES], 0.0), axis=1, keepdims=True)
            y_ref[0, r0:r0 + LANES, :] = (acc_ref[r0:r0 + LANES, :] * col).astype(BF16)


def _moe_ffn(xe, gate_row, w_in, w_out, layer):
    E, M, D = xe.shape
    FF = w_out.shape[2]
    tf = FF_TILE
    nf = FF // tf
    return pl.pallas_call(
        _moe_ffn_kernel,
        grid=(E, nf),
        in_specs=[
            pl.BlockSpec((1, M, D), lambda e, f: (e, 0, 0)),
            pl.BlockSpec((1, 1, M), lambda e, f: (e, 0, 0)),
            pl.BlockSpec((1, 1, D, tf), lambda e, f: (layer, e, 0, f)),
            pl.BlockSpec((1, 1, D, tf), lambda e, f: (layer, e, 0, nf + f)),
            pl.BlockSpec((1, 1, tf, D), lambda e, f: (layer, e, f, 0)),
        ],
        out_specs=pl.BlockSpec((1, M, D), lambda e, f: (e, 0, 0)),
        out_shape=jax.ShapeDtypeStruct((E, M, D), BF16),
        scratch_shapes=[pltpu.VMEM((M, D), F32)],
        compiler_params=_cparams("parallel", "arbitrary"),
        name="moe_ffn",
    )(xe, gate_row, w_in, w_in, w_out)


def _combine_ln_kernel(win_ref, pos_ref, y_ref, x_ref, mod_ref, g_ref, b_ref, o_ref, *, window):
    b = pl.program_id(0)
    j = pl.program_id(1)
    n_exp = y_ref.shape[0]
    tiles = pos_ref.shape[1] // LANES
    slot = lax.broadcasted_iota(jnp.int32, (LANES, window), 1)
    m = mod_ref[0]
    for t in range(tiles):
        rows = slice(t * LANES, (t + 1) * LANES)
        pos = pos_ref[0, rows, :]
        tile = (b * pl.num_programs(1) + j) * tiles + t
        f = None
        for e in range(n_exp):
            start = pl.multiple_of(win_ref[tile * n_exp + e], SLOT_ALIGN)
            onehot = jnp.where(pos[:, e:e + 1] - start == slot, 1.0, 0.0).astype(BF16)
            part = _dot(onehot, y_ref[e, pl.ds(start, window), :])
            f = part if f is None else f + part
        o_ref[rows, :] = _layer_norm(ALPHA * x_ref[rows, :] + (1.0 + m[5:6]) * f, g_ref[...], b_ref[...])


def _combine_ln(pos_t, win, y, x1, mods3, ln_g, ln_b, B, S, cap, window):
    E, _, D = y.shape
    tm = min(COMBINE_TILE, S)
    nj = S // tm
    row = pl.BlockSpec((tm, D), lambda b, j, w: (b * nj + j, 0))
    vec = pl.BlockSpec((1, D), lambda b, j, w: (0, 0))
    return pl.pallas_call(
        functools.partial(_combine_ln_kernel, window=window),
        grid_spec=pltpu.PrefetchScalarGridSpec(
            num_scalar_prefetch=1,
            grid=(B, nj),
            in_specs=[
                pl.BlockSpec((1, tm, E), lambda b, j, w: (b, j, 0)),
                pl.BlockSpec((E, cap, D), lambda b, j, w: (0, b, 0)),
                row,
                pl.BlockSpec((1, 6, D), lambda b, j, w: (b, 0, 0)),
                vec,
                vec,
            ],
            out_specs=row,
        ),
        out_shape=jax.ShapeDtypeStruct((B * S, D), F32),
        compiler_params=_cparams("parallel", "arbitrary"),
        name="moe_combine_ln",
    )(win, pos_t, y, x1, mods3, ln_g.reshape(1, D), ln_b.reshape(1, D))


def _rope_tables(positions):
    B, S = positions.shape
    pos = positions.astype(F32).reshape(B * S, 1)
    inv_m = MLA_THETA ** (-jnp.arange(0, MLA_ROPE, 2, dtype=F32) / MLA_ROPE)
    inv_d = ROPE_THETA ** (-jnp.arange(0, DIFF_ROT, 2, dtype=F32) / DIFF_ROT)
    rest = jnp.zeros((LANES - MLA_ROPE - DIFF_ROT,), F32)
    ang = pos * jnp.concatenate([inv_m, inv_m, inv_d, inv_d, rest])
    return jnp.cos(ang), jnp.sin(ang)


def _moe_block(x1, h2, aff, mods3, w_in, w_out, layer, ln_g, ln_b, B, S):
    E = N_EXPERTS
    cap = max(1, EC_CAPACITY_FACTOR * S // E)
    idx, gate, pos, off = _topk(aff, cap)
    gate_row = gate[:, :, 0, :].transpose(1, 0, 2).reshape(E, 1, B * cap)
    xe = _dispatch(idx, h2, B, S, cap)
    y = _moe_ffn(xe, gate_row, w_in, w_out, layer)
    window = min(COMBINE_WINDOW, cap)
    pos_t = pos.reshape(B, E, S).transpose(0, 2, 1)
    win = jnp.minimum(off[:, :, :, 0] // SLOT_ALIGN * SLOT_ALIGN, cap - window)
    win = win.transpose(0, 2, 1).reshape(-1)
    return _combine_ln(pos_t, win, y, x1, mods3, ln_g, ln_b, B, S, cap, window)


def kernel(x, c, positions, ada_w, ada_b, ln1_g, ln1_b, ln2_g, ln2_b, mla_w_in, mla_q_norm, mla_kv_norm,
           mla_w_uq, mla_w_ukv, mla_w_o, diff_w_in, diff_lambda, diff_subln, diff_w_o, router_w, moe_w_in,
           moe_w_out):
    B, S, D = x.shape
    cos_t, sin_t = _rope_tables(positions)
    mods = _mods(c, ada_w, ada_b)
    x2 = x.reshape(B * S, D)
    for i in range(DEPTH):
        mods3 = mods[i, :B].reshape(B, 6, D)
        j = i // 2
        if i % 2 == 0:
            q, k, v = _mla_proj(x2, mods3, cos_t, sin_t, mla_w_in[j], mla_q_norm[j], mla_kv_norm[j],
                                mla_w_uq[j], mla_w_ukv[j], B, S)
            o = _attention(_mla_attn_kernel, [], 1, q, k, v, "mla_attn")
            w_o = mla_w_o[j]
        else:
            lambda_init = 0.8 - 0.6 * math.exp(-0.3 * i)
            q, k, v = _diff_proj(x2, mods3, cos_t, sin_t, diff_w_in[j], B, S)
            o = _attention(functools.partial(_diff_attn_kernel, lambda_init=lambda_init),
                           [diff_lambda[j], diff_subln[j].reshape(DIFF_V, 1)], 2, q, k, v, "diff_attn")
            w_o = diff_w_o[j]
        x1, h2, aff = _post_attn(o.reshape(B * S, D), x2, mods3, w_o, ln1_g[i], ln1_b[i], router_w[i], B, S)
        x2 = _moe_block(x1, h2, aff, mods3, moe_w_in, moe_w_out, i, ln2_g[i], ln2_b[i], B, S)
    return x2.reshape(B, S, D)
```

```python
import functools
import math

import jax
import jax.numpy as jnp
from jax import lax
from jax.experimental import pallas as pl
from jax.experimental.pallas import tpu as pltpu

F32 = jnp.float32
BF16 = jnp.bfloat16

D_MODEL = 1024
DEPTH = 2
MLA_HEADS = 8
MLA_Q_LORA = 384
MLA_KV_LORA = 256
MLA_NOPE = 128
MLA_ROPE = 64
MLA_V = 128
MLA_THETA = 10000.0
DIFF_HEADS = 8
DIFF_HEAD_DIM = 64
DIFF_V = 2 * DIFF_HEAD_DIM
DIFF_ROT = DIFF_HEAD_DIM // 4
ROPE_THETA = 500000.0
N_EXPERTS = 16
EXPERT_FF = 2048
EC_CAPACITY_FACTOR = 2
NORM_EPS = 1e-5
LATENT_EPS = 1e-6
ALPHA = (2 * DEPTH) ** 0.25

LANES = 128
VMEM_LIMIT = 56 * 1024 * 1024

TOKEN_TILE = 512
PROJ_SUB_ROWS = 256
POST_TILE = 1024
POST_SUB_ROWS = 256
KEY_CHUNK = 512
ROW_BLOCK = 256
LOG2E = math.log2(math.e)
FF_TILE = 512
FFN_OUT_TILE = 512
FFN_SUB_ROWS = 512
SLOT_ALIGN = 16
COMBINE_WINDOW = 256
COMBINE_TILE = 512


def _cparams(*sem):
    return pltpu.CompilerParams(dimension_semantics=sem, vmem_limit_bytes=VMEM_LIMIT)


def _dot(a, b):
    return jnp.dot(a, b, preferred_element_type=F32)


def _split_bf16(a):
    hi = a.astype(BF16)
    lo = (a - hi.astype(F32)).astype(BF16)
    return hi, lo


def _dot3(a, b):
    ah, al = _split_bf16(a)
    bh, bl = _split_bf16(b)
    return _dot(ah, bh) + (_dot(ah, bl) + _dot(al, bh))


def _layer_norm(y, g, b):
    mu = jnp.mean(y, axis=-1, keepdims=True)
    yc = y - mu
    var = jnp.mean(yc * yc, axis=-1, keepdims=True)
    return yc * lax.rsqrt(var + NORM_EPS) * g + b


def _mods_kernel(c_ref, w_ref, b_ref, o_ref):
    c = c_ref[...]
    a = c * (1.0 / (1.0 + jnp.exp(-c)))
    o_ref[0] = _dot3(a, w_ref[0]) + b_ref[0]


def _mods(c, ada_w, ada_b):
    B, D = c.shape
    L, _, N = ada_w.shape
    rows = 8
    cp = jnp.zeros((rows, D), F32).at[:B].set(c)
    tn = 1536
    return pl.pallas_call(
        _mods_kernel,
        grid=(L, N // tn),
        in_specs=[
            pl.BlockSpec((rows, D), lambda l, j: (0, 0)),
            pl.BlockSpec((1, D, tn), lambda l, j: (l, 0, j)),
            pl.BlockSpec((1, 1, tn), lambda l, j: (l, 0, j)),
        ],
        out_specs=pl.BlockSpec((1, rows, tn), lambda l, j: (l, 0, j)),
        out_shape=jax.ShapeDtypeStruct((L, rows, N), F32),
        compiler_params=_cparams("parallel", "parallel"),
        name="adaln_mods",
    )(cp, ada_w, ada_b.reshape(L, 1, N))


def _mla_proj_kernel(x_ref, mod_ref, cos_ref, sin_ref, win_ref, qn_ref, kvn_ref, wq_ref, wkv_ref,
                     q_ref, k_ref, v_ref, *, scale):
    m = mod_ref[0]
    half = MLA_ROPE // 2
    hw = MLA_HEADS * LANES
    for r0 in range(0, x_ref.shape[0], PROJ_SUB_ROWS):
        rows = slice(r0, r0 + PROJ_SUB_ROWS)
        h = (x_ref[rows, :] * (1.0 + m[1:2]) + m[0:1]).astype(BF16)
        z = _dot(h, win_ref[...])
        cq = z[:, :MLA_Q_LORA]
        ckv = z[:, MLA_Q_LORA:MLA_Q_LORA + MLA_KV_LORA]
        kr = z[:, MLA_Q_LORA + MLA_KV_LORA:]
        lane = lax.broadcasted_iota(jnp.int32, kr.shape, 1)
        cos = cos_ref[rows, :]
        sin = sin_ref[rows, :]
        sin_up = jnp.where(lane < half, -sin, 0.0)
        sin_dn = jnp.where((lane >= half) & (lane < MLA_ROPE), sin, 0.0)

        def rope(t):
            return t * cos + pltpu.roll(t, LANES - half, 1) * sin_up + pltpu.roll(t, half, 1) * sin_dn

        qn = (cq * lax.rsqrt(jnp.mean(cq * cq, axis=-1, keepdims=True) + LATENT_EPS) * qn_ref[...]).astype(BF16)
        kvn = (ckv * lax.rsqrt(jnp.mean(ckv * ckv, axis=-1, keepdims=True) + LATENT_EPS)
               * kvn_ref[...]).astype(BF16)
        q = _dot(qn, wq_ref[...])
        kv = _dot(kvn, wkv_ref[...])
        kpe = rope(kr).astype(BF16)
        for hh in range(MLA_HEADS):
            a, b = hh * LANES, (hh + 1) * LANES
            q_ref[0, hh, rows, 0:LANES] = (q[:, a:b] * scale).astype(BF16)
            q_ref[0, hh, rows, LANES:2 * LANES] = (rope(q[:, hw + a:hw + b]) * scale).astype(BF16)
            k_ref[0, hh, rows, 0:LANES] = kv[:, a:b].astype(BF16)
            k_ref[0, hh, rows, LANES:2 * LANES] = kpe
            v_ref[0, hh, rows, :] = kv[:, hw + a:hw + b].astype(BF16)


def _mla_proj(x2, mods3, cos_t, sin_t, w_in, q_norm, kv_norm, w_uq, w_ukv, B, S):
    T, D = x2.shape
    H = MLA_HEADS
    tm = min(TOKEN_TILE, S)
    npb = S // tm
    win_p = jnp.pad(w_in, ((0, 0), (0, LANES - MLA_ROPE))).astype(BF16)
    wq3 = w_uq.reshape(MLA_Q_LORA, H, MLA_NOPE + MLA_ROPE)
    rope_p = jnp.pad(wq3[:, :, MLA_NOPE:], ((0, 0), (0, 0), (0, LANES - MLA_ROPE))).reshape(MLA_Q_LORA, H * LANES)
    wq_p = jnp.concatenate([wq3[:, :, :MLA_NOPE].reshape(MLA_Q_LORA, H * MLA_NOPE), rope_p],
                           axis=1).astype(BF16)
    wkv3 = w_ukv.reshape(MLA_KV_LORA, H, MLA_NOPE + MLA_V)
    wkv_p = jnp.concatenate([wkv3[:, :, :MLA_NOPE].reshape(MLA_KV_LORA, H * MLA_NOPE),
                             wkv3[:, :, MLA_NOPE:].reshape(MLA_KV_LORA, H * MLA_V)], axis=1).astype(BF16)
    scale = (MLA_NOPE + MLA_ROPE) ** -0.5 * LOG2E
    full = lambda shape: pl.BlockSpec(shape, lambda i: (0,) * len(shape))
    return pl.pallas_call(
        functools.partial(_mla_proj_kernel, scale=scale),
        grid=(T // tm,),
        in_specs=[
            pl.BlockSpec((tm, D), lambda i: (i, 0)),
            pl.BlockSpec((1, 6, D), lambda i: (i // npb, 0, 0)),
            pl.BlockSpec((tm, LANES), lambda i: (i, 0)),
            pl.BlockSpec((tm, LANES), lambda i: (i, 0)),
            full(win_p.shape),
            full((1, MLA_Q_LORA)),
            full((1, MLA_KV_LORA)),
            full(wq_p.shape),
            full(wkv_p.shape),
        ],
        out_specs=[
            pl.BlockSpec((1, H, tm, 2 * LANES), lambda i: (i // npb, 0, i % npb, 0)),
            pl.BlockSpec((1, H, tm, 2 * LANES), lambda i: (i // npb, 0, i % npb, 0)),
            pl.BlockSpec((1, H, tm, MLA_V), lambda i: (i // npb, 0, i % npb, 0)),
        ],
        out_shape=[
            jax.ShapeDtypeStruct((B, H, S, 2 * LANES), BF16),
            jax.ShapeDtypeStruct((B, H, S, 2 * LANES), BF16),
            jax.ShapeDtypeStruct((B, H, S, MLA_V), BF16),
        ],
        compiler_params=_cparams("parallel"),
        name="mla_proj",
    )(x2, mods3, cos_t, sin_t, win_p, q_norm.reshape(1, -1), kv_norm.reshape(1, -1), wq_p, wkv_p)


_NT = (((1,), (1,)), ((), ()))


def _attn_pipeline(load_q, n_maps, k_ref, vt_ref, st_ref, m_ref, emit, n_blocks):
    S = k_ref.shape[2]
    tk = min(KEY_CHUNK, S)
    chunks = [slice(c * tk, (c + 1) * tk) for c in range(S // tk)]

    def scores(blk, slot):
        for mp, qb in enumerate(load_q(blk)):
            m = None
            for ck in chunks:
                sj = lax.dot_general(k_ref[0, 0, ck, :], qb, _NT, preferred_element_type=F32)
                st_ref[slot, mp, ck, :] = sj
                mj = jnp.max(sj, axis=0, keepdims=True)
                m = mj if m is None else jnp.maximum(m, mj)
            m_ref[slot, mp] = m

    def attend(blk, slot):
        res = []
        for mp in range(n_maps):
            m = m_ref[slot, mp]
            l = None
            acc = None
            for ck in chunks:
                e = jnp.exp2(st_ref[slot, mp, ck, :] - m)
                lj = jnp.sum(e, axis=0, keepdims=True)
                l = lj if l is None else l + lj
                pv = _dot(vt_ref[:, ck], e.astype(BF16))
                acc = pv if acc is None else acc + pv
            res.append((acc, l))
        emit(blk, res)

    scores(0, 0)

    def body(jj, carry):
        j = 2 * jj
        for u in range(2):
            scores(jnp.minimum(j + u + 1, n_blocks - 1), (u + 1) % 2)
            attend(j + u, u % 2)
        return carry

    lax.fori_loop(0, n_blocks // 2, body, 0)


def _block_rows(blk):
    return pl.ds(pl.multiple_of(blk * ROW_BLOCK, ROW_BLOCK), ROW_BLOCK)


def _mla_attn_kernel(q_ref, k_ref, v_ref, o_ref, vt_ref, st_ref, m_ref):
    vt_ref[...] = v_ref[0, 0].astype(F32).T.astype(BF16)

    def emit(blk, res):
        (acc, l), = res
        o_ref[0, _block_rows(blk), :] = (acc * (1.0 / l)).T.astype(BF16)

    _attn_pipeline(lambda blk: [q_ref[0, 0, _block_rows(blk), :]], 1, k_ref, vt_ref, st_ref, m_ref, emit,
                   q_ref.shape[2] // ROW_BLOCK)


def _diff_attn_kernel(lam_ref, subln_ref, q_ref, k_ref, v_ref, o_ref, vt_ref, st_ref, m_ref, *, lambda_init):
    vt_ref[...] = v_ref[0, 0].astype(F32).T.astype(BF16)
    lf = lam_ref[...]
    lam = (jnp.exp(jnp.sum(lf[0:1] * lf[1:2], axis=-1, keepdims=True))
           - jnp.exp(jnp.sum(lf[2:3] * lf[3:4], axis=-1, keepdims=True)) + lambda_init)

    def load_q(blk):
        q = q_ref[0, 0, _block_rows(blk), :]
        lane = lax.broadcasted_iota(jnp.int32, q.shape, 1)
        zero = jnp.zeros_like(q)
        return [jnp.where(lane < DIFF_HEAD_DIM, q, zero), jnp.where(lane >= DIFF_HEAD_DIM, q, zero)]

    def emit(blk, res):
        (acc1, l1), (acc2, l2) = res
        o = acc1 * (1.0 / l1) - acc2 * (lam / l2)
        o = o * lax.rsqrt(jnp.mean(o * o, axis=0, keepdims=True) + NORM_EPS) * subln_ref[...]
        o_ref[0, _block_rows(blk), :] = (o * (1.0 - lambda_init)).T.astype(BF16)

    _attn_pipeline(load_q, 2, k_ref, vt_ref, st_ref, m_ref, emit, q_ref.shape[2] // ROW_BLOCK)


def _attention(kernel_fn, extra, n_maps, q, k, v, name):
    B, H, S, dq = q.shape
    dv = v.shape[-1]
    assert (S // ROW_BLOCK) % 2 == 0
    extra_specs = [pl.BlockSpec(a.shape, lambda b, h, n=a.ndim: (0,) * n) for a in extra]
    return pl.pallas_call(
        kernel_fn,
        grid=(B, H),
        in_specs=extra_specs + [
            pl.BlockSpec((1, 1, S, dq), lambda b, h: (b, h, 0, 0)),
            pl.BlockSpec((1, 1, S, dq), lambda b, h: (b, h, 0, 0)),
            pl.BlockSpec((1, 1, S, dv), lambda b, h: (b, h, 0, 0)),
        ],
        out_specs=pl.BlockSpec((1, S, dv), lambda b, h: (b, 0, h)),
        out_shape=jax.ShapeDtypeStruct((B, S, H * dv), BF16),
        scratch_shapes=[
            pltpu.VMEM((dv, S), BF16),
            pltpu.VMEM((2, n_maps, S, ROW_BLOCK), F32),
            pltpu.VMEM((2, n_maps, 1, ROW_BLOCK), F32),
        ],
        compiler_params=_cparams("parallel", "parallel"),
        name=name,
    )(*extra, q, k, v)


def _diff_proj_kernel(x_ref, mod_ref, cos_ref, sin_ref, win_ref, q_ref, k_ref, v_ref, *, scale):
    m = mod_ref[0]
    h = (x_ref[...] * (1.0 + m[1:2]) + m[0:1]).astype(BF16)
    z = _dot(h, win_ref[...])
    cos_t = cos_ref[...]
    sin_t = sin_ref[...]
    lane = lax.broadcasted_iota(jnp.int32, cos_t.shape, 1)
    low = lane < DIFF_HEAD_DIM
    r = lane % DIFF_HEAD_DIM
    hr = DIFF_ROT // 2
    c = jnp.where(r < DIFF_ROT, jnp.where(low, pltpu.roll(cos_t, DIFF_HEAD_DIM, 1), cos_t), 1.0)
    s = jnp.where(low, pltpu.roll(sin_t, DIFF_HEAD_DIM, 1), sin_t)
    sa = jnp.where(r < hr, -s, 0.0)
    sb = jnp.where((r >= hr) & (r < DIFF_ROT), s, 0.0)
    hw = DIFF_HEADS * LANES

    def rope(t):
        return t * c + pltpu.roll(t, LANES - DIFF_ROT // 2, 1) * sa + pltpu.roll(t, DIFF_ROT // 2, 1) * sb

    for hh in range(DIFF_HEADS):
        a, b = hh * LANES, (hh + 1) * LANES
        q_ref[0, hh] = (rope(z[:, a:b]) * scale).astype(BF16)
        k_ref[0, hh] = rope(z[:, hw + a:hw + b]).astype(BF16)
        v_ref[0, hh] = z[:, 2 * hw + a:2 * hw + b].astype(BF16)


def _diff_proj(x2, mods3, cos_t, sin_t, w_in, B, S):
    T, D = x2.shape
    H = DIFF_HEADS
    tm = min(TOKEN_TILE, S)
    npb = S // tm
    win = w_in.astype(BF16)
    hd = jax.ShapeDtypeStruct((B, H, S, LANES), BF16)
    hspec = pl.BlockSpec((1, H, tm, LANES), lambda i: (i // npb, 0, i % npb, 0))
    return pl.pallas_call(
        functools.partial(_diff_proj_kernel, scale=DIFF_HEAD_DIM ** -0.5 * LOG2E),
        grid=(T // tm,),
        in_specs=[
            pl.BlockSpec((tm, D), lambda i: (i, 0)),
            pl.BlockSpec((1, 6, D), lambda i: (i // npb, 0, 0)),
            pl.BlockSpec((tm, LANES), lambda i: (i, 0)),
            pl.BlockSpec((tm, LANES), lambda i: (i, 0)),
            pl.BlockSpec(win.shape, lambda i: (0, 0)),
        ],
        out_specs=[hspec, hspec, hspec],
        out_shape=[hd, hd, hd],
        compiler_params=_cparams("parallel"),
        name="diff_proj",
    )(x2, mods3, cos_t, sin_t, win)


def _post_attn_kernel(o_ref, x_ref, mod_ref, wo_ref, g_ref, b_ref, rw_ref, x1_ref, h2_ref, aff_ref):
    m = mod_ref[0]
    w_hi, w_lo = _split_bf16(rw_ref[...])
    rw = jnp.concatenate([w_hi, w_lo], axis=1)
    for r0 in range(0, o_ref.shape[0], POST_SUB_ROWS):
        rows = slice(r0, r0 + POST_SUB_ROWS)
        t = _dot(o_ref[rows, :], wo_ref[...])
        x1 = _layer_norm(ALPHA * x_ref[rows, :] + (1.0 + m[2:3]) * t, g_ref[...], b_ref[...])
        x1_ref[rows, :] = x1
        h2 = x1 * (1.0 + m[4:5]) + m[3:4]
        h2_ref[rows, :] = h2
        h_hi, h_lo = _split_bf16(h2)
        hw = _dot(h_hi, rw)
        logits = hw[:, :LANES] + (hw[:, LANES:] + _dot(h_lo, w_hi))
        lt = logits.T[:N_EXPERTS]
        mx = jnp.max(lt, axis=0, keepdims=True)
        ex = jnp.exp(lt - mx)
        aff_ref[0, :, rows] = ex / jnp.sum(ex, axis=0, keepdims=True)


def _post_attn(o2, x2, mods3, w_o, ln_g, ln_b, router_w, B, S):
    T, D = x2.shape
    E = N_EXPERTS
    tm = min(POST_TILE, S)
    npb = S // tm
    rw = jnp.pad(router_w, ((0, 0), (0, LANES - E)))
    return pl.pallas_call(
        _post_attn_kernel,
        grid=(T // tm,),
        in_specs=[
            pl.BlockSpec((tm, D), lambda i: (i, 0)),
            pl.BlockSpec((tm, D), lambda i: (i, 0)),
            pl.BlockSpec((1, 6, D), lambda i: (i // npb, 0, 0)),
            pl.BlockSpec((D, D), lambda i: (0, 0)),
            pl.BlockSpec((1, D), lambda i: (0, 0)),
            pl.BlockSpec((1, D), lambda i: (0, 0)),
            pl.BlockSpec((D, LANES), lambda i: (0, 0)),
        ],
        out_specs=[
            pl.BlockSpec((tm, D), lambda i: (i, 0)),
            pl.BlockSpec((tm, D), lambda i: (i, 0)),
            pl.BlockSpec((1, E, tm), lambda i: (i // npb, 0, i % npb)),
        ],
        out_shape=[
            jax.ShapeDtypeStruct((T, D), F32),
            jax.ShapeDtypeStruct((T, D), F32),
            jax.ShapeDtypeStruct((B, E, S), F32),
        ],
        compiler_params=_cparams("parallel"),
        name="post_attn",
    )(o2, x2, mods3, w_o.astype(BF16), ln_g.reshape(1, D), ln_b.reshape(1, D), rw)


def _topk_kernel(aff_ref, idx_ref, gate_ref, pos_ref, off_ref, *, cap):
    a3 = aff_ref[0]
    E, NC, _ = a3.shape
    R = E * NC

    def count3(mask3):
        c = jnp.sum(jnp.where(mask3, 1.0, 0.0), axis=2, keepdims=True)
        return jnp.sum(c, axis=1, keepdims=True)

    def body(i, bits):
        cand = bits | jnp.left_shift(jnp.int32(1), 30 - i)
        reach = count3(a3 >= lax.bitcast_convert_type(cand, F32))
        return jnp.where(reach >= cap, cand, bits)

    thr = lax.bitcast_convert_type(lax.fori_loop(0, 31, body, jnp.zeros((E, 1, 1), jnp.int32)), F32)
    gt3 = a3 > thr
    eq3 = a3 == thr
    need = cap - count3(gt3)

    li = lax.broadcasted_iota(jnp.int32, (LANES, LANES), 0)
    lj = lax.broadcasted_iota(jnp.int32, (LANES, LANES), 1)
    tri_excl = jnp.where(li < lj, 1.0, 0.0).astype(BF16)
    tri_incl = jnp.where(li <= lj, 1.0, 0.0).astype(BF16)
    ri = lax.broadcasted_iota(jnp.int32, (R, R), 0)
    rj = lax.broadcasted_iota(jnp.int32, (R, R), 1)
    same = (ri // NC) == (rj // NC)
    blk_lower = jnp.where(same & (rj < ri), 1.0, 0.0).astype(BF16)

    def chunk_offsets(mask2):
        tot = jnp.sum(mask2, axis=1, keepdims=True)
        totb = jnp.broadcast_to(tot, (R, LANES)).astype(BF16)
        return _dot(blk_lower, totb), tot

    eq2 = jnp.where(eq3, 1.0, 0.0).reshape(R, LANES)
    eq_off, _ = chunk_offsets(eq2)
    eq_rank = (eq_off + _dot(eq2.astype(BF16), tri_excl)).reshape(E, NC, LANES)
    sel3 = gt3 | (eq3 & (eq_rank < need))
    sel2 = jnp.where(sel3, 1.0, 0.0).reshape(R, LANES)
    off2, tot2 = chunk_offsets(sel2)
    cum2 = _dot(sel2.astype(BF16), tri_incl)
    pos_ref[0] = jnp.where(sel3, (off2 + cum2 - 1.0).reshape(E, NC, LANES), -1.0).astype(jnp.int32)
    off_ref[0] = off2.reshape(E, NC, LANES).astype(jnp.int32)
    a2 = a3.reshape(R, LANES)
    a_hi = a2.astype(BF16)
    a_mid = (a2 - a_hi.astype(F32)).astype(BF16)
    a_lo = (a2 - a_hi.astype(F32) - a_mid.astype(F32)).astype(BF16)

    cs = lax.broadcasted_iota(jnp.int32, (NC, cap), 1).astype(F32)
    ks = lax.broadcasted_iota(jnp.int32, (NC, cap), 0).astype(F32)
    ls = lax.broadcasted_iota(jnp.int32, (LANES, cap), 0).astype(F32)
    for e in range(E):
        rows = slice(e * NC, (e + 1) * NC)
        off_e = off2[rows, 0:1]
        end_e = off_e + tot2[rows]
        k_of_c = jnp.sum(jnp.where(end_e <= cs, 1.0, 0.0), axis=0, keepdims=True)
        onehot = ks == k_of_c
        oh = jnp.where(onehot, 1.0, 0.0).astype(BF16)
        off_c = jnp.sum(jnp.where(onehot, off_e, 0.0), axis=0, keepdims=True)
        rank_c = cs[0:1] - off_c
        cum_c = _dot(cum2[rows].T.astype(BF16), oh)
        lane_c = jnp.sum(jnp.where(cum_c <= rank_c, 1.0, 0.0), axis=0, keepdims=True)
        idx_ref[0, e] = (k_of_c * LANES + lane_c).astype(jnp.int32)
        g_c = (_dot(a_hi[rows].T, oh) + _dot(a_mid[rows].T, oh)) + _dot(a_lo[rows].T, oh)
        gate_ref[0, e] = jnp.sum(jnp.where(ls == lane_c, g_c, 0.0), axis=0, keepdims=True)


def _topk(aff, cap):
    B, E, S = aff.shape
    NC = S // LANES
    slot_spec = pl.BlockSpec((1, E, 1, cap), lambda b: (b, 0, 0, 0))
    tok_spec = pl.BlockSpec((1, E, NC, LANES), lambda b: (b, 0, 0, 0))
    return pl.pallas_call(
        functools.partial(_topk_kernel, cap=cap),
        grid=(B,),
        in_specs=[tok_spec],
        out_specs=[slot_spec, slot_spec, tok_spec, tok_spec],
        out_shape=[
            jax.ShapeDtypeStruct((B, E, 1, cap), jnp.int32),
            jax.ShapeDtypeStruct((B, E, 1, cap), F32),
            jax.ShapeDtypeStruct((B, E, NC, LANES), jnp.int32),
            jax.ShapeDtypeStruct((B, E, NC, LANES), jnp.int32),
        ],
        compiler_params=_cparams("parallel"),
        name="ec_topk",
    )(aff.reshape(B, E, NC, LANES))


def _dispatch_kernel(idx_ref, h_ref, xe_ref, rows_ref):
    cap = xe_ref.shape[1]

    def group(g, carry):
        for r in range(SLOT_ALIGN):
            rows_ref[r:r + 1, :] = h_ref[pl.ds(idx_ref[0, 0, g * SLOT_ALIGN + r], 1), :]
        start = pl.multiple_of(g * SLOT_ALIGN, SLOT_ALIGN)
        xe_ref[0, pl.ds(start, SLOT_ALIGN), :] = rows_ref[...].astype(BF16)
        return carry

    lax.fori_loop(0, cap // SLOT_ALIGN, group, 0)


def _dispatch(idx, h2, B, S, cap):
    E = idx.shape[1]
    D = h2.shape[1]
    return pl.pallas_call(
        _dispatch_kernel,
        grid=(B, E),
        in_specs=[
            pl.BlockSpec((1, 1, cap), lambda b, e: (b * E + e, 0, 0), memory_space=pltpu.SMEM),
            pl.BlockSpec((S, D), lambda b, e: (b, 0)),
        ],
        out_specs=pl.BlockSpec((1, cap, D), lambda b, e: (e, b, 0)),
        out_shape=jax.ShapeDtypeStruct((E, B * cap, D), BF16),
        scratch_shapes=[pltpu.VMEM((SLOT_ALIGN, D), F32)],
        compiler_params=_cparams("parallel", "parallel"),
        name="moe_dispatch",
    )(idx.reshape(B * E, 1, cap), h2)


def _moe_ffn_kernel(xe_ref, gate_ref, wg_ref, wu_ref, wo_ref, y_ref, act_ref, *, n_hidden_steps):
    s = pl.program_id(1)
    tf = wg_ref.shape[3]

    for f in range(n_hidden_steps):
        @pl.when(s == f)
        def _():
            wg = wg_ref[0, 0].astype(BF16)
            wu = wu_ref[0, 0].astype(BF16)
            for r0 in range(0, xe_ref.shape[1], FFN_SUB_ROWS):
                rows = slice(r0, r0 + FFN_SUB_ROWS)
                xe = xe_ref[0, rows, :]
                g = _dot(xe, wg)
                u = _dot(xe, wu)
                act_ref[rows, f * tf:(f + 1) * tf] = (g * (1.0 / (1.0 + jnp.exp(-g))) * u).astype(BF16)

    @pl.when(s >= n_hidden_steps)
    def _():
        out = _dot(act_ref[...], wo_ref[0, 0].astype(BF16))
        diag = (lax.broadcasted_iota(jnp.int32, (LANES, LANES), 0)
                == lax.broadcasted_iota(jnp.int32, (LANES, LANES), 1))
        for r0 in range(0, out.shape[0], LANES):
            col = jnp.sum(jnp.where(diag, gate_ref[0, :, r0:r0 + LANES], 0.0), axis=1, keepdims=True)
            y_ref[0, r0:r0 + LANES, :] = (out[r0:r0 + LANES, :] * col).astype(BF16)


def _moe_ffn(xe, gate_row, w_in, w_out, layer):
    E, M, D = xe.shape
    FF = w_out.shape[2]
    tf = FF_TILE
    nf = FF // tf
    tn = FFN_OUT_TILE
    nn = D // tn
    hid = lambda s: jnp.minimum(s, nf - 1)
    col = lambda s: jnp.maximum(s - nf, 0)
    return pl.pallas_call(
        functools.partial(_moe_ffn_kernel, n_hidden_steps=nf),
        grid=(E, nf + nn),
        in_specs=[
            pl.BlockSpec((1, M, D), lambda e, s: (e, 0, 0)),
            pl.BlockSpec((1, 1, M), lambda e, s: (e, 0, 0)),
            pl.BlockSpec((1, 1, D, tf), lambda e, s: (layer, e, 0, hid(s))),
            pl.BlockSpec((1, 1, D, tf), lambda e, s: (layer, e, 0, nf + hid(s))),
            pl.BlockSpec((1, 1, FF, tn), lambda e, s: (layer, e, 0, col(s))),
        ],
        out_specs=pl.BlockSpec((1, M, tn), lambda e, s: (e, 0, col(s))),
        out_shape=jax.ShapeDtypeStruct((E, M, D), BF16),
        scratch_shapes=[pltpu.VMEM((M, FF), BF16)],
        compiler_params=_cparams("parallel", "arbitrary"),
        name="moe_ffn",
    )(xe, gate_row, w_in, w_in, w_out)


def _combine_ln_kernel(win_ref, pos_ref, y_ref, x_ref, mod_ref, g_ref, b_ref, o_ref, *, window):
    b = pl.program_id(0)
    j = pl.program_id(1)
    n_exp = y_ref.shape[0]
    tiles = pos_ref.shape[1] // LANES
    slot = lax.broadcasted_iota(jnp.int32, (LANES, window), 1)
    m = mod_ref[0]
    for t in range(tiles):
        rows = slice(t * LANES, (t + 1) * LANES)
        pos = pos_ref[0, rows, :]
        tile = (b * pl.num_programs(1) + j) * tiles + t
        f = None
        for e in range(n_exp):
            start = pl.multiple_of(win_ref[tile * n_exp + e], SLOT_ALIGN)
            onehot = jnp.where(pos[:, e:e + 1] - start == slot, 1.0, 0.0).astype(BF16)
            part = _dot(onehot, y_ref[e, pl.ds(start, window), :])
            f = part if f is None else f + part
        o_ref[rows, :] = _layer_norm(ALPHA * x_ref[rows, :] + (1.0 + m[5:6]) * f, g_ref[...], b_ref[...])


def _combine_ln(pos_t, win, y, x1, mods3, ln_g, ln_b, B, S, cap, window):
    E, _, D = y.shape
    tm = min(COMBINE_TILE, S)
    nj = S // tm
    row = pl.BlockSpec((tm, D), lambda b, j, w: (b * nj + j, 0))
    vec = pl.BlockSpec((1, D), lambda b, j, w: (0, 0))
    return pl.pallas_call(
        functools.partial(_combine_ln_kernel, window=window),
        grid_spec=pltpu.PrefetchScalarGridSpec(
            num_scalar_prefetch=1,
            grid=(B, nj),
            in_specs=[
                pl.BlockSpec((1, tm, E), lambda b, j, w: (b, j, 0)),
                pl.BlockSpec((E, cap, D), lambda b, j, w: (0, b, 0)),
                row,
                pl.BlockSpec((1, 6, D), lambda b, j, w: (b, 0, 0)),
                vec,
                vec,
            ],
            out_specs=row,
        ),
        out_shape=jax.ShapeDtypeStruct((B * S, D), F32),
        compiler_params=_cparams("parallel", "arbitrary"),
        name="moe_combine_ln",
    )(win, pos_t, y, x1, mods3, ln_g.reshape(1, D), ln_b.reshape(1, D))


def _rope_tables(positions):
    B, S = positions.shape
    pos = positions.astype(F32).reshape(B * S, 1)
    inv_m = MLA_THETA ** (-jnp.arange(0, MLA_ROPE, 2, dtype=F32) / MLA_ROPE)
    inv_d = ROPE_THETA ** (-jnp.arange(0, DIFF_ROT, 2, dtype=F32) / DIFF_ROT)
    rest = jnp.zeros((LANES - MLA_ROPE - DIFF_ROT,), F32)
    ang = pos * jnp.concatenate([inv_m, inv_m, inv_d, inv_d, rest])
    return jnp.cos(ang), jnp.sin(ang)


def _moe_block(x1, h2, aff, mods3, w_in, w_out, layer, ln_g, ln_b, B, S):
    E = N_EXPERTS
    cap = max(1, EC_CAPACITY_FACTOR * S // E)
    idx, gate, pos, off = _topk(aff, cap)
    gate_row = gate[:, :, 0, :].transpose(1, 0, 2).reshape(E, 1, B * cap)
    xe = _dispatch(idx, h2, B, S, cap)
    y = _moe_ffn(xe, gate_row, w_in, w_out, layer)
    window = min(COMBINE_WINDOW, cap)
    pos_t = pos.reshape(B, E, S).transpose(0, 2, 1)
    win = jnp.minimum(off[:, :, :, 0] // SLOT_ALIGN * SLOT_ALIGN, cap - window)
    win = win.transpose(0, 2, 1).reshape(-1)
    return _combine_ln(pos_t, win, y, x1, mods3, ln_g, ln_b, B, S, cap, window)


def kernel(x, c, positions, ada_w, ada_b, ln1_g, ln1_b, ln2_g, ln2_b, mla_w_in, mla_q_norm, mla_kv_norm,
           mla_w_uq, mla_w_ukv, mla_w_o, diff_w_in, diff_lambda, diff_subln, diff_w_o, router_w, moe_w_in,
           moe_w_out):
    B, S, D = x.shape
    cos_t, sin_t = _rope_tables(positions)
    mods = _mods(c, ada_w, ada_b)
    x2 = x.reshape(B * S, D)
    for i in range(DEPTH):
        mods3 = mods[i, :B].reshape(B, 6, D)
        j = i // 2
        if i % 2 == 0:
            q, k, v = _mla_proj(x2, mods3, cos_t, sin_t, mla_w_in[j], mla_q_norm[j], mla_kv_norm[j],
                                mla_w_uq[j], mla_w_ukv[j], B, S)
            o = _attention(_mla_attn_kernel, [], 1, q, k, v, "mla_attn")
            w_o = mla_w_o[j]
        else:
            lambda_init = 0.8 - 0.6 * math.exp(-0.3 * i)
            q, k, v = _diff_proj(x2, mods3, cos_t, sin_t, diff_w_in[j], B, S)
            o = _attention(functools.partial(_diff_attn_kernel, lambda_init=lambda_init),
                           [diff_lambda[j], diff_subln[j].reshape(DIFF_V, 1)], 2, q, k, v, "diff_attn")
            w_o = diff_w_o[j]
        x1, h2, aff = _post_attn(o.reshape(B * S, D), x2, mods3, w_o, ln1_g[i], ln1_b[i], router_w[i], B, S)
        x2 = _moe_block(x1, h2, aff, mods3, moe_w_in, moe_w_out, i, ln2_g[i], ln2_b[i], B, S)
    return x2.reshape(B, S, D)
```

```python
import functools
import math

import jax
import jax.numpy as jnp
from jax import lax
from jax.experimental import pallas as pl
from jax.experimental.pallas import tpu as pltpu

F32 = jnp.float32
BF16 = jnp.bfloat16

D_MODEL = 1024
DEPTH = 2
MLA_HEADS = 8
MLA_Q_LORA = 384
MLA_KV_LORA = 256
MLA_NOPE = 128
MLA_ROPE = 64
MLA_V = 128
MLA_THETA = 10000.0
DIFF_HEADS = 8
DIFF_HEAD_DIM = 64
DIFF_V = 2 * DIFF_HEAD_DIM
DIFF_ROT = DIFF_HEAD_DIM // 4
ROPE_THETA = 500000.0
N_EXPERTS = 16
EXPERT_FF = 2048
EC_CAPACITY_FACTOR = 2
NORM_EPS = 1e-5
LATENT_EPS = 1e-6
ALPHA = (2 * DEPTH) ** 0.25

LANES = 128
VMEM_LIMIT = 56 * 1024 * 1024

TOKEN_TILE = 512
PROJ_SUB_ROWS = 256
POST_TILE = 1024
POST_SUB_ROWS = 256
KEY_CHUNK = 512
ROW_BLOCK = 256
LOG2E = math.log2(math.e)
FF_TILE = 512
FFN_OUT_TILE = 512
FFN_SUB_ROWS = 512
SLOT_ALIGN = 16
COMBINE_WINDOW = 256
COMBINE_TILE = 512


def _cparams(*sem):
    return pltpu.CompilerParams(dimension_semantics=sem, vmem_limit_bytes=VMEM_LIMIT)


def _dot(a, b):
    return jnp.dot(a, b, preferred_element_type=F32)


def _split_bf16(a):
    hi = a.astype(BF16)
    lo = (a - hi.astype(F32)).astype(BF16)
    return hi, lo


def _dot3(a, b):
    ah, al = _split_bf16(a)
    bh, bl = _split_bf16(b)
    return _dot(ah, bh) + (_dot(ah, bl) + _dot(al, bh))


def _layer_norm(y, g, b):
    mu = jnp.mean(y, axis=-1, keepdims=True)
    yc = y - mu
    var = jnp.mean(yc * yc, axis=-1, keepdims=True)
    return yc * lax.rsqrt(var + NORM_EPS) * g + b


def _mods_kernel(c_ref, w_ref, b_ref, o_ref):
    c = c_ref[...]
    a = c * (1.0 / (1.0 + jnp.exp(-c)))
    o_ref[0] = _dot3(a, w_ref[0]) + b_ref[0]


def _mods(c, ada_w, ada_b):
    B, D = c.shape
    L, _, N = ada_w.shape
    rows = 8
    cp = jnp.zeros((rows, D), F32).at[:B].set(c)
    tn = 1536
    return pl.pallas_call(
        _mods_kernel,
        grid=(L, N // tn),
        in_specs=[
            pl.BlockSpec((rows, D), lambda l, j: (0, 0)),
            pl.BlockSpec((1, D, tn), lambda l, j: (l, 0, j)),
            pl.BlockSpec((1, 1, tn), lambda l, j: (l, 0, j)),
        ],
        out_specs=pl.BlockSpec((1, rows, tn), lambda l, j: (l, 0, j)),
        out_shape=jax.ShapeDtypeStruct((L, rows, N), F32),
        compiler_params=_cparams("parallel", "parallel"),
        name="adaln_mods",
    )(cp, ada_w, ada_b.reshape(L, 1, N))


def _mla_proj_kernel(x_ref, mod_ref, cos_ref, sin_ref, win_ref, qn_ref, kvn_ref, wq_ref, wkv_ref,
                     q_ref, k_ref, v_ref, *, scale):
    m = mod_ref[0]
    half = MLA_ROPE // 2
    hw = MLA_HEADS * LANES
    for r0 in range(0, x_ref.shape[0], PROJ_SUB_ROWS):
        rows = slice(r0, r0 + PROJ_SUB_ROWS)
        h = (x_ref[rows, :] * (1.0 + m[1:2]) + m[0:1]).astype(BF16)
        z = _dot(h, win_ref[...])
        cq = z[:, :MLA_Q_LORA]
        ckv = z[:, MLA_Q_LORA:MLA_Q_LORA + MLA_KV_LORA]
        kr = z[:, MLA_Q_LORA + MLA_KV_LORA:]
        lane = lax.broadcasted_iota(jnp.int32, kr.shape, 1)
        cos = cos_ref[rows, :]
        sin = sin_ref[rows, :]
        sin_up = jnp.where(lane < half, -sin, 0.0)
        sin_dn = jnp.where((lane >= half) & (lane < MLA_ROPE), sin, 0.0)

        def rope(t):
            return t * cos + pltpu.roll(t, LANES - half, 1) * sin_up + pltpu.roll(t, half, 1) * sin_dn

        qn = (cq * lax.rsqrt(jnp.mean(cq * cq, axis=-1, keepdims=True) + LATENT_EPS) * qn_ref[...]).astype(BF16)
        kvn = (ckv * lax.rsqrt(jnp.mean(ckv * ckv, axis=-1, keepdims=True) + LATENT_EPS)
               * kvn_ref[...]).astype(BF16)
        q = _dot(qn, wq_ref[...])
        kv = _dot(kvn, wkv_ref[...])
        kpe = rope(kr).astype(BF16)
        for hh in range(MLA_HEADS):
            a, b = hh * LANES, (hh + 1) * LANES
            q_ref[0, hh, rows, 0:LANES] = (q[:, a:b] * scale).astype(BF16)
            q_ref[0, hh, rows, LANES:2 * LANES] = (rope(q[:, hw + a:hw + b]) * scale).astype(BF16)
            k_ref[0, hh, rows, 0:LANES] = kv[:, a:b].astype(BF16)
            k_ref[0, hh, rows, LANES:2 * LANES] = kpe
            v_ref[0, hh, rows, :] = kv[:, hw + a:hw + b].astype(BF16)


def _mla_proj(x2, mods3, cos_t, sin_t, w_in, q_norm, kv_norm, w_uq, w_ukv, B, S):
    T, D = x2.shape
    H = MLA_HEADS
    tm = min(TOKEN_TILE, S)
    npb = S // tm
    win_p = jnp.pad(w_in, ((0, 0), (0, LANES - MLA_ROPE))).astype(BF16)
    wq3 = w_uq.reshape(MLA_Q_LORA, H, MLA_NOPE + MLA_ROPE)
    rope_p = jnp.pad(wq3[:, :, MLA_NOPE:], ((0, 0), (0, 0), (0, LANES - MLA_ROPE))).reshape(MLA_Q_LORA, H * LANES)
    wq_p = jnp.concatenate([wq3[:, :, :MLA_NOPE].reshape(MLA_Q_LORA, H * MLA_NOPE), rope_p],
                           axis=1).astype(BF16)
    wkv3 = w_ukv.reshape(MLA_KV_LORA, H, MLA_NOPE + MLA_V)
    wkv_p = jnp.concatenate([wkv3[:, :, :MLA_NOPE].reshape(MLA_KV_LORA, H * MLA_NOPE),
                             wkv3[:, :, MLA_NOPE:].reshape(MLA_KV_LORA, H * MLA_V)], axis=1).astype(BF16)
    scale = (MLA_NOPE + MLA_ROPE) ** -0.5 * LOG2E
    full = lambda shape: pl.BlockSpec(shape, lambda i: (0,) * len(shape))
    return pl.pallas_call(
        functools.partial(_mla_proj_kernel, scale=scale),
        grid=(T // tm,),
        in_specs=[
            pl.BlockSpec((tm, D), lambda i: (i, 0)),
            pl.BlockSpec((1, 6, D), lambda i: (i // npb, 0, 0)),
            pl.BlockSpec((tm, LANES), lambda i: (i, 0)),
            pl.BlockSpec((tm, LANES), lambda i: (i, 0)),
            full(win_p.shape),
            full((1, MLA_Q_LORA)),
            full((1, MLA_KV_LORA)),
            full(wq_p.shape),
            full(wkv_p.shape),
        ],
        out_specs=[
            pl.BlockSpec((1, H, tm, 2 * LANES), lambda i: (i // npb, 0, i % npb, 0)),
            pl.BlockSpec((1, H, tm, 2 * LANES), lambda i: (i // npb, 0, i % npb, 0)),
            pl.BlockSpec((1, H, tm, MLA_V), lambda i: (i // npb, 0, i % npb, 0)),
        ],
        out_shape=[
            jax.ShapeDtypeStruct((B, H, S, 2 * LANES), BF16),
            jax.ShapeDtypeStruct((B, H, S, 2 * LANES), BF16),
            jax.ShapeDtypeStruct((B, H, S, MLA_V), BF16),
        ],
        compiler_params=_cparams("parallel"),
        name="mla_proj",
    )(x2, mods3, cos_t, sin_t, win_p, q_norm.reshape(1, -1), kv_norm.reshape(1, -1), wq_p, wkv_p)


_NT = (((1,), (1,)), ((), ()))


def _attn_pipeline(load_q, n_maps, k_ref, vt_ref, st_ref, m_ref, emit, n_blocks):
    S = k_ref.shape[2]
    tk = min(KEY_CHUNK, S)
    chunks = [slice(c * tk, (c + 1) * tk) for c in range(S // tk)]

    def scores(blk, slot):
        for mp, qb in enumerate(load_q(blk)):
            m = None
            for ck in chunks:
                sj = lax.dot_general(k_ref[0, 0, ck, :], qb, _NT, preferred_element_type=F32)
                st_ref[slot, mp, ck, :] = sj
                mj = jnp.max(sj, axis=0, keepdims=True)
                m = mj if m is None else jnp.maximum(m, mj)
            m_ref[slot, mp] = m

    def attend(blk, slot):
        res = []
        for mp in range(n_maps):
            m = m_ref[slot, mp]
            l = None
            acc = None
            for ck in chunks:
                e = jnp.exp2(st_ref[slot, mp, ck, :] - m)
                lj = jnp.sum(e, axis=0, keepdims=True)
                l = lj if l is None else l + lj
                pv = _dot(vt_ref[:, ck], e.astype(BF16))
                acc = pv if acc is None else acc + pv
            res.append((acc, l))
        emit(blk, res)

    scores(0, 0)

    def body(jj, carry):
        j = 2 * jj
        for u in range(2):
            scores(jnp.minimum(j + u + 1, n_blocks - 1), (u + 1) % 2)
            attend(j + u, u % 2)
        return carry

    lax.fori_loop(0, n_blocks // 2, body, 0)


def _block_rows(blk):
    return pl.ds(pl.multiple_of(blk * ROW_BLOCK, ROW_BLOCK), ROW_BLOCK)


def _mla_attn_kernel(q_ref, k_ref, v_ref, o_ref, vt_ref, st_ref, m_ref):
    vt_ref[...] = v_ref[0, 0].astype(F32).T.astype(BF16)

    def emit(blk, res):
        (acc, l), = res
        o_ref[0, _block_rows(blk), :] = (acc * (1.0 / l)).T.astype(BF16)

    _attn_pipeline(lambda blk: [q_ref[0, 0, _block_rows(blk), :]], 1, k_ref, vt_ref, st_ref, m_ref, emit,
                   q_ref.shape[2] // ROW_BLOCK)


def _diff_attn_kernel(lam_ref, subln_ref, q_ref, k_ref, v_ref, o_ref, vt_ref, st_ref, m_ref, *, lambda_init):
    vt_ref[...] = v_ref[0, 0].astype(F32).T.astype(BF16)
    lf = lam_ref[...]
    lam = (jnp.exp(jnp.sum(lf[0:1] * lf[1:2], axis=-1, keepdims=True))
           - jnp.exp(jnp.sum(lf[2:3] * lf[3:4], axis=-1, keepdims=True)) + lambda_init)

    def load_q(blk):
        q = q_ref[0, 0, _block_rows(blk), :]
        lane = lax.broadcasted_iota(jnp.int32, q.shape, 1)
        zero = jnp.zeros_like(q)
        return [jnp.where(lane < DIFF_HEAD_DIM, q, zero), jnp.where(lane >= DIFF_HEAD_DIM, q, zero)]

    def emit(blk, res):
        (acc1, l1), (acc2, l2) = res
        o = acc1 * (1.0 / l1) - acc2 * (lam / l2)
        o = o * lax.rsqrt(jnp.mean(o * o, axis=0, keepdims=True) + NORM_EPS) * subln_ref[...]
        o_ref[0, _block_rows(blk), :] = (o * (1.0 - lambda_init)).T.astype(BF16)

    _attn_pipeline(load_q, 2, k_ref, vt_ref, st_ref, m_ref, emit, q_ref.shape[2] // ROW_BLOCK)


def _attention(kernel_fn, extra, n_maps, q, k, v, name):
    B, H, S, dq = q.shape
    dv = v.shape[-1]
    assert (S // ROW_BLOCK) % 2 == 0
    extra_specs = [pl.BlockSpec(a.shape, lambda b, h, n=a.ndim: (0,) * n) for a in extra]
    return pl.pallas_call(
        kernel_fn,
        grid=(B, H),
        in_specs=extra_specs + [
            pl.BlockSpec((1, 1, S, dq), lambda b, h: (b, h, 0, 0)),
            pl.BlockSpec((1, 1, S, dq), lambda b, h: (b, h, 0, 0)),
            pl.BlockSpec((1, 1, S, dv), lambda b, h: (b, h, 0, 0)),
        ],
        out_specs=pl.BlockSpec((1, S, dv), lambda b, h: (b, 0, h)),
        out_shape=jax.ShapeDtypeStruct((B, S, H * dv), BF16),
        scratch_shapes=[
            pltpu.VMEM((dv, S), BF16),
            pltpu.VMEM((2, n_maps, S, ROW_BLOCK), F32),
            pltpu.VMEM((2, n_maps, 1, ROW_BLOCK), F32),
        ],
        compiler_params=_cparams("parallel", "parallel"),
        name=name,
    )(*extra, q, k, v)


def _diff_proj_kernel(x_ref, mod_ref, cos_ref, sin_ref, win_ref, q_ref, k_ref, v_ref, *, scale):
    m = mod_ref[0]
    h = (x_ref[...] * (1.0 + m[1:2]) + m[0:1]).astype(BF16)
    z = _dot(h, win_ref[...])
    cos_t = cos_ref[...]
    sin_t = sin_ref[...]
    lane = lax.broadcasted_iota(jnp.int32, cos_t.shape, 1)
    low = lane < DIFF_HEAD_DIM
    r = lane % DIFF_HEAD_DIM
    hr = DIFF_ROT // 2
    c = jnp.where(r < DIFF_ROT, jnp.where(low, pltpu.roll(cos_t, DIFF_HEAD_DIM, 1), cos_t), 1.0)
    s = jnp.where(low, pltpu.roll(sin_t, DIFF_HEAD_DIM, 1), sin_t)
    sa = jnp.where(r < hr, -s, 0.0)
    sb = jnp.where((r >= hr) & (r < DIFF_ROT), s, 0.0)
    hw = DIFF_HEADS * LANES

    def rope(t):
        return t * c + pltpu.roll(t, LANES - DIFF_ROT // 2, 1) * sa + pltpu.roll(t, DIFF_ROT // 2, 1) * sb

    for hh in range(DIFF_HEADS):
        a, b = hh * LANES, (hh + 1) * LANES
        q_ref[0, hh] = (rope(z[:, a:b]) * scale).astype(BF16)
        k_ref[0, hh] = rope(z[:, hw + a:hw + b]).astype(BF16)
        v_ref[0, hh] = z[:, 2 * hw + a:2 * hw + b].astype(BF16)


def _diff_proj(x2, mods3, cos_t, sin_t, w_in, B, S):
    T, D = x2.shape
    H = DIFF_HEADS
    tm = min(TOKEN_TILE, S)
    npb = S // tm
    win = w_in.astype(BF16)
    hd = jax.ShapeDtypeStruct((B, H, S, LANES), BF16)
    hspec = pl.BlockSpec((1, H, tm, LANES), lambda i: (i // npb, 0, i % npb, 0))
    return pl.pallas_call(
        functools.partial(_diff_proj_kernel, scale=DIFF_HEAD_DIM ** -0.5 * LOG2E),
        grid=(T // tm,),
        in_specs=[
            pl.BlockSpec((tm, D), lambda i: (i, 0)),
            pl.BlockSpec((1, 6, D), lambda i: (i // npb, 0, 0)),
            pl.BlockSpec((tm, LANES), lambda i: (i, 0)),
            pl.BlockSpec((tm, LANES), lambda i: (i, 0)),
            pl.BlockSpec(win.shape, lambda i: (0, 0)),
        ],
        out_specs=[hspec, hspec, hspec],
        out_shape=[hd, hd, hd],
        compiler_params=_cparams("parallel"),
        name="diff_proj",
    )(x2, mods3, cos_t, sin_t, win)


def _post_attn_kernel(o_ref, x_ref, mod_ref, wo_ref, g_ref, b_ref, rw_ref, x1_ref, aff_ref):
    m = mod_ref[0]
    w_hi, w_lo = _split_bf16(rw_ref[...])
    rw = jnp.concatenate([w_hi, w_lo], axis=1)
    for r0 in range(0, o_ref.shape[0], POST_SUB_ROWS):
        rows = slice(r0, r0 + POST_SUB_ROWS)
        t = _dot(o_ref[rows, :], wo_ref[...])
        x1 = _layer_norm(ALPHA * x_ref[rows, :] + (1.0 + m[2:3]) * t, g_ref[...], b_ref[...])
        x1_ref[rows, :] = x1
        h2 = x1 * (1.0 + m[4:5]) + m[3:4]
        h_hi, h_lo = _split_bf16(h2)
        hw = _dot(h_hi, rw)
        logits = hw[:, :LANES] + (hw[:, LANES:] + _dot(h_lo, w_hi))
        lt = logits.T[:N_EXPERTS]
        mx = jnp.max(lt, axis=0, keepdims=True)
        ex = jnp.exp(lt - mx)
        aff_ref[0, :, rows] = ex / jnp.sum(ex, axis=0, keepdims=True)


def _post_attn(o2, x2, mods3, w_o, ln_g, ln_b, router_w, B, S):
    T, D = x2.shape
    E = N_EXPERTS
    tm = min(POST_TILE, S)
    npb = S // tm
    rw = jnp.pad(router_w, ((0, 0), (0, LANES - E)))
    return pl.pallas_call(
        _post_attn_kernel,
        grid=(T // tm,),
        in_specs=[
            pl.BlockSpec((tm, D), lambda i: (i, 0)),
            pl.BlockSpec((tm, D), lambda i: (i, 0)),
            pl.BlockSpec((1, 6, D), lambda i: (i // npb, 0, 0)),
            pl.BlockSpec((D, D), lambda i: (0, 0)),
            pl.BlockSpec((1, D), lambda i: (0, 0)),
            pl.BlockSpec((1, D), lambda i: (0, 0)),
            pl.BlockSpec((D, LANES), lambda i: (0, 0)),
        ],
        out_specs=[
            pl.BlockSpec((tm, D), lambda i: (i, 0)),
            pl.BlockSpec((1, E, tm), lambda i: (i // npb, 0, i % npb)),
        ],
        out_shape=[
            jax.ShapeDtypeStruct((T, D), F32),
            jax.ShapeDtypeStruct((B, E, S), F32),
        ],
        compiler_params=_cparams("parallel"),
        name="post_attn",
    )(o2, x2, mods3, w_o.astype(BF16), ln_g.reshape(1, D), ln_b.reshape(1, D), rw)


def _topk_kernel(aff_ref, idx_ref, gate_ref, pos_ref, off_ref, *, cap):
    a3 = aff_ref[0]
    E, NC, _ = a3.shape
    R = E * NC

    def count3(mask3):
        c = jnp.sum(jnp.where(mask3, 1.0, 0.0), axis=1, keepdims=True)
        return jnp.sum(c, axis=2, keepdims=True)

    def body(i, bits):
        cand = bits | jnp.left_shift(jnp.int32(1), 30 - i)
        reach = count3(a3 >= lax.bitcast_convert_type(cand, F32))
        return jnp.where(reach >= cap, cand, bits)

    thr = lax.bitcast_convert_type(lax.fori_loop(0, 31, body, jnp.zeros((E, 1, 1), jnp.int32)), F32)
    gt3 = a3 > thr
    eq3 = a3 == thr
    need = cap - count3(gt3)

    li = lax.broadcasted_iota(jnp.int32, (LANES, LANES), 0)
    lj = lax.broadcasted_iota(jnp.int32, (LANES, LANES), 1)
    tri_excl = jnp.where(li < lj, 1.0, 0.0).astype(BF16)
    tri_incl = jnp.where(li <= lj, 1.0, 0.0).astype(BF16)
    ri = lax.broadcasted_iota(jnp.int32, (R, R), 0)
    rj = lax.broadcasted_iota(jnp.int32, (R, R), 1)
    same = (ri // NC) == (rj // NC)
    blk_lower = jnp.where(same & (rj < ri), 1.0, 0.0).astype(BF16)

    def chunk_offsets(mask2):
        tot = jnp.sum(mask2, axis=1, keepdims=True)
        totb = jnp.broadcast_to(tot, (R, LANES)).astype(BF16)
        return _dot(blk_lower, totb), tot

    eq2 = jnp.where(eq3, 1.0, 0.0).reshape(R, LANES)
    eq_off, _ = chunk_offsets(eq2)
    eq_rank = (eq_off + _dot(eq2.astype(BF16), tri_excl)).reshape(E, NC, LANES)
    sel3 = gt3 | (eq3 & (eq_rank < need))
    sel2 = jnp.where(sel3, 1.0, 0.0).reshape(R, LANES)
    off2, tot2 = chunk_offsets(sel2)
    cum2 = _dot(sel2.astype(BF16), tri_incl)
    pos_ref[0] = jnp.where(sel3, (off2 + cum2 - 1.0).reshape(E, NC, LANES), -1.0).astype(jnp.int32)
    off_ref[0] = off2.reshape(E, NC, LANES).astype(jnp.int32)
    a2 = a3.reshape(R, LANES)
    a_hi = a2.astype(BF16)
    a_mid = (a2 - a_hi.astype(F32)).astype(BF16)
    a_lo = (a2 - a_hi.astype(F32) - a_mid.astype(F32)).astype(BF16)

    cs = lax.broadcasted_iota(jnp.int32, (NC, cap), 1).astype(F32)
    ks = lax.broadcasted_iota(jnp.int32, (NC, cap), 0).astype(F32)
    ls = lax.broadcasted_iota(jnp.int32, (LANES, cap), 0).astype(F32)
    for e in range(E):
        rows = slice(e * NC, (e + 1) * NC)
        off_e = off2[rows, 0:1]
        end_e = off_e + tot2[rows]
        k_of_c = jnp.sum(jnp.where(end_e <= cs, 1.0, 0.0), axis=0, keepdims=True)
        onehot = ks == k_of_c
        oh = jnp.where(onehot, 1.0, 0.0).astype(BF16)
        off_c = jnp.sum(jnp.where(onehot, off_e, 0.0), axis=0, keepdims=True)
        rank_c = cs[0:1] - off_c
        cum_c = _dot(cum2[rows].T.astype(BF16), oh)
        lane_c = jnp.sum(jnp.where(cum_c <= rank_c, 1.0, 0.0), axis=0, keepdims=True)
        idx_ref[0, e] = (k_of_c * LANES + lane_c).astype(jnp.int32)
        g_c = (_dot(a_hi[rows].T, oh) + _dot(a_mid[rows].T, oh)) + _dot(a_lo[rows].T, oh)
        gate_ref[0, e] = jnp.sum(jnp.where(ls == lane_c, g_c, 0.0), axis=0, keepdims=True)


def _topk(aff, cap):
    B, E, S = aff.shape
    NC = S // LANES
    slot_spec = pl.BlockSpec((1, E, 1, cap), lambda b: (b, 0, 0, 0))
    tok_spec = pl.BlockSpec((1, E, NC, LANES), lambda b: (b, 0, 0, 0))
    return pl.pallas_call(
        functools.partial(_topk_kernel, cap=cap),
        grid=(B,),
        in_specs=[tok_spec],
        out_specs=[slot_spec, slot_spec, tok_spec, tok_spec],
        out_shape=[
            jax.ShapeDtypeStruct((B, E, 1, cap), jnp.int32),
            jax.ShapeDtypeStruct((B, E, 1, cap), F32),
            jax.ShapeDtypeStruct((B, E, NC, LANES), jnp.int32),
            jax.ShapeDtypeStruct((B, E, NC, LANES), jnp.int32),
        ],
        compiler_params=_cparams("parallel"),
        name="ec_topk",
    )(aff.reshape(B, E, NC, LANES))


def _dispatch_kernel(idx_ref, x_ref, mod_ref, xe_ref, rows_ref):
    cap = xe_ref.shape[1]
    m = mod_ref[0]
    scale = 1.0 + m[4:5]
    shift = m[3:4]

    def group(g, carry):
        for r in range(SLOT_ALIGN):
            rows_ref[r:r + 1, :] = x_ref[pl.ds(idx_ref[0, 0, g * SLOT_ALIGN + r], 1), :]
        start = pl.multiple_of(g * SLOT_ALIGN, SLOT_ALIGN)
        xe_ref[0, pl.ds(start, SLOT_ALIGN), :] = (rows_ref[...] * scale + shift).astype(BF16)
        return carry

    lax.fori_loop(0, cap // SLOT_ALIGN, group, 0)


def _dispatch(idx, x1, mods3, B, S, cap):
    E = idx.shape[1]
    D = x1.shape[1]
    return pl.pallas_call(
        _dispatch_kernel,
        grid=(B, E),
        in_specs=[
            pl.BlockSpec((1, 1, cap), lambda b, e: (b * E + e, 0, 0), memory_space=pltpu.SMEM),
            pl.BlockSpec((S, D), lambda b, e: (b, 0)),
            pl.BlockSpec((1, 6, D), lambda b, e: (b, 0, 0)),
        ],
        out_specs=pl.BlockSpec((1, cap, D), lambda b, e: (e, b, 0)),
        out_shape=jax.ShapeDtypeStruct((E, B * cap, D), BF16),
        scratch_shapes=[pltpu.VMEM((SLOT_ALIGN, D), F32)],
        compiler_params=_cparams("parallel", "parallel"),
        name="moe_dispatch",
    )(idx.reshape(B * E, 1, cap), x1, mods3)


def _moe_ffn_kernel(xe_ref, gate_ref, wg_ref, wu_ref, wo_ref, y_ref, act_ref, *, n_hidden_steps):
    s = pl.program_id(1)
    tf = wg_ref.shape[3]

    for f in range(n_hidden_steps):
        @pl.when(s == f)
        def _():
            wg = wg_ref[0, 0].astype(BF16)
            wu = wu_ref[0, 0].astype(BF16)
            for r0 in range(0, xe_ref.shape[1], FFN_SUB_ROWS):
                rows = slice(r0, r0 + FFN_SUB_ROWS)
                xe = xe_ref[0, rows, :]
                g = _dot(xe, wg)
                u = _dot(xe, wu)
                act_ref[rows, f * tf:(f + 1) * tf] = (g * (1.0 / (1.0 + jnp.exp(-g))) * u).astype(BF16)

    @pl.when(s >= n_hidden_steps)
    def _():
        out = _dot(act_ref[...], wo_ref[0, 0].astype(BF16))
        diag = (lax.broadcasted_iota(jnp.int32, (LANES, LANES), 0)
                == lax.broadcasted_iota(jnp.int32, (LANES, LANES), 1))
        for r0 in range(0, out.shape[0], LANES):
            col = jnp.sum(jnp.where(diag, gate_ref[0, :, r0:r0 + LANES], 0.0), axis=1, keepdims=True)
            y_ref[0, r0:r0 + LANES, :] = (out[r0:r0 + LANES, :] * col).astype(BF16)


def _moe_ffn(xe, gate_row, w_in, w_out, layer):
    E, M, D = xe.shape
    FF = w_out.shape[2]
    tf = FF_TILE
    nf = FF // tf
    tn = FFN_OUT_TILE
    nn = D // tn
    hid = lambda s: jnp.minimum(s, nf - 1)
    col = lambda s: jnp.maximum(s - nf, 0)
    return pl.pallas_call(
        functools.partial(_moe_ffn_kernel, n_hidden_steps=nf),
        grid=(E, nf + nn),
        in_specs=[
            pl.BlockSpec((1, M, D), lambda e, s: (e, 0, 0)),
            pl.BlockSpec((1, 1, M), lambda e, s: (e, 0, 0)),
            pl.BlockSpec((1, 1, D, tf), lambda e, s: (layer, e, 0, hid(s))),
            pl.BlockSpec((1, 1, D, tf), lambda e, s: (layer, e, 0, nf + hid(s))),
            pl.BlockSpec((1, 1, FF, tn), lambda e, s: (layer, e, 0, col(s))),
        ],
        out_specs=pl.BlockSpec((1, M, tn), lambda e, s: (e, 0, col(s))),
        out_shape=jax.ShapeDtypeStruct((E, M, D), BF16),
        scratch_shapes=[pltpu.VMEM((M, FF), BF16)],
        compiler_params=_cparams("parallel", "arbitrary"),
        name="moe_ffn",
    )(xe, gate_row, w_in, w_in, w_out)


def _combine_ln_kernel(win_ref, pos_ref, y_ref, x_ref, mod_ref, g_ref, b_ref, o_ref, *, window):
    b = pl.program_id(0)
    j = pl.program_id(1)
    n_exp = y_ref.shape[0]
    tiles = pos_ref.shape[1] // LANES
    slot = lax.broadcasted_iota(jnp.int32, (LANES, window), 1)
    m = mod_ref[0]
    for t in range(tiles):
        rows = slice(t * LANES, (t + 1) * LANES)
        pos = pos_ref[0, rows, :]
        tile = (b * pl.num_programs(1) + j) * tiles + t
        f = None
        for e in range(n_exp):
            start = pl.multiple_of(win_ref[tile * n_exp + e], SLOT_ALIGN)
            onehot = jnp.where(pos[:, e:e + 1] - start == slot, 1.0, 0.0).astype(BF16)
            part = _dot(onehot, y_ref[e, pl.ds(start, window), :])
            f = part if f is None else f + part
        o_ref[rows, :] = _layer_norm(ALPHA * x_ref[rows, :] + (1.0 + m[5:6]) * f, g_ref[...], b_ref[...])


def _combine_ln(pos_t, win, y, x1, mods3, ln_g, ln_b, B, S, cap, window):
    E, _, D = y.shape
    tm = min(COMBINE_TILE, S)
    nj = S // tm
    row = pl.BlockSpec((tm, D), lambda b, j, w: (b * nj + j, 0))
    vec = pl.BlockSpec((1, D), lambda b, j, w: (0, 0))
    return pl.pallas_call(
        functools.partial(_combine_ln_kernel, window=window),
        grid_spec=pltpu.PrefetchScalarGridSpec(
            num_scalar_prefetch=1,
            grid=(B, nj),
            in_specs=[
                pl.BlockSpec((1, tm, E), lambda b, j, w: (b, j, 0)),
                pl.BlockSpec((E, cap, D), lambda b, j, w: (0, b, 0)),
                row,
                pl.BlockSpec((1, 6, D), lambda b, j, w: (b, 0, 0)),
                vec,
                vec,
            ],
            out_specs=row,
        ),
        out_shape=jax.ShapeDtypeStruct((B * S, D), F32),
        compiler_params=_cparams("parallel", "arbitrary"),
        name="moe_combine_ln",
    )(win, pos_t, y, x1, mods3, ln_g.reshape(1, D), ln_b.reshape(1, D))


def _rope_tables(positions):
    B, S = positions.shape
    pos = positions.astype(F32).reshape(B * S, 1)
    inv_m = MLA_THETA ** (-jnp.arange(0, MLA_ROPE, 2, dtype=F32) / MLA_ROPE)
    inv_d = ROPE_THETA ** (-jnp.arange(0, DIFF_ROT, 2, dtype=F32) / DIFF_ROT)
    rest = jnp.zeros((LANES - MLA_ROPE - DIFF_ROT,), F32)
    ang = pos * jnp.concatenate([inv_m, inv_m, inv_d, inv_d, rest])
    return jnp.cos(ang), jnp.sin(ang)


def _moe_block(x1, aff, mods3, w_in, w_out, layer, ln_g, ln_b, B, S):
    E = N_EXPERTS
    cap = max(1, EC_CAPACITY_FACTOR * S // E)
    idx, gate, pos, off = _topk(aff, cap)
    gate_row = gate[:, :, 0, :].transpose(1, 0, 2).reshape(E, 1, B * cap)
    xe = _dispatch(idx, x1, mods3, B, S, cap)
    y = _moe_ffn(xe, gate_row, w_in, w_out, layer)
    window = min(COMBINE_WINDOW, cap)
    pos_t = pos.reshape(B, E, S).transpose(0, 2, 1)
    win = jnp.minimum(off[:, :, :, 0] // SLOT_ALIGN * SLOT_ALIGN, cap - window)
    win = win.transpose(0, 2, 1).reshape(-1)
    return _combine_ln(pos_t, win, y, x1, mods3, ln_g, ln_b, B, S, cap, window)


def kernel(x, c, positions, ada_w, ada_b, ln1_g, ln1_b, ln2_g, ln2_b, mla_w_in, mla_q_norm, mla_kv_norm,
           mla_w_uq, mla_w_ukv, mla_w_o, diff_w_in, diff_lambda, diff_subln, diff_w_o, router_w, moe_w_in,
           moe_w_out):
    B, S, D = x.shape
    cos_t, sin_t = _rope_tables(positions)
    mods = _mods(c, ada_w, ada_b)
    x2 = x.reshape(B * S, D)
    for i in range(DEPTH):
        mods3 = mods[i, :B].reshape(B, 6, D)
        j = i // 2
        if i % 2 == 0:
            q, k, v = _mla_proj(x2, mods3, cos_t, sin_t, mla_w_in[j], mla_q_norm[j], mla_kv_norm[j],
                                mla_w_uq[j], mla_w_ukv[j], B, S)
            o = _attention(_mla_attn_kernel, [], 1, q, k, v, "mla_attn")
            w_o = mla_w_o[j]
        else:
            lambda_init = 0.8 - 0.6 * math.exp(-0.3 * i)
            q, k, v = _diff_proj(x2, mods3, cos_t, sin_t, diff_w_in[j], B, S)
            o = _attention(functools.partial(_diff_attn_kernel, lambda_init=lambda_init),
                           [diff_lambda[j], diff_subln[j].reshape(DIFF_V, 1)], 2, q, k, v, "diff_attn")
            w_o = diff_w_o[j]
        x1, aff = _post_attn(o.reshape(B * S, D), x2, mods3, w_o, ln1_g[i], ln1_b[i], router_w[i], B, S)
        x2 = _moe_block(x1, aff, mods3, moe_w_in, moe_w_out, i, ln2_g[i], ln2_b[i], B, S)
    return x2.reshape(B, S, D)
```

```python
import functools
import math

import jax
import jax.numpy as jnp
from jax import lax
from jax.experimental import pallas as pl
from jax.experimental.pallas import tpu as pltpu

F32 = jnp.float32
BF16 = jnp.bfloat16

D_MODEL = 1024
DEPTH = 2
MLA_HEADS = 8
MLA_Q_LORA = 384
MLA_KV_LORA = 256
MLA_NOPE = 128
MLA_ROPE = 64
MLA_V = 128
MLA_THETA = 10000.0
DIFF_HEADS = 8
DIFF_HEAD_DIM = 64
DIFF_V = 2 * DIFF_HEAD_DIM
DIFF_ROT = DIFF_HEAD_DIM // 4
ROPE_THETA = 500000.0
N_EXPERTS = 16
EXPERT_FF = 2048
EC_CAPACITY_FACTOR = 2
NORM_EPS = 1e-5
LATENT_EPS = 1e-6
ALPHA = (2 * DEPTH) ** 0.25

LANES = 128
VMEM_LIMIT = 56 * 1024 * 1024

TOKEN_TILE = 512
PROJ_SUB_ROWS = 256
POST_TILE = 1024
POST_SUB_ROWS = 256
KEY_CHUNK = 512
ROW_BLOCK = 256
LOG2E = math.log2(math.e)
FF_TILE = 512
FFN_OUT_TILE = 512
FFN_SUB_ROWS = 512
SLOT_ALIGN = 16
COMBINE_WINDOW = 256
COMBINE_TILE = 512


def _cparams(*sem):
    return pltpu.CompilerParams(dimension_semantics=sem, vmem_limit_bytes=VMEM_LIMIT)


def _dot(a, b):
    return jnp.dot(a, b, preferred_element_type=F32)


def _split_bf16(a):
    hi = a.astype(BF16)
    lo = (a - hi.astype(F32)).astype(BF16)
    return hi, lo


def _dot3(a, b):
    ah, al = _split_bf16(a)
    bh, bl = _split_bf16(b)
    return _dot(ah, bh) + (_dot(ah, bl) + _dot(al, bh))


def _layer_norm(y, g, b):
    mu = jnp.mean(y, axis=-1, keepdims=True)
    yc = y - mu
    var = jnp.mean(yc * yc, axis=-1, keepdims=True)
    return yc * lax.rsqrt(var + NORM_EPS) * g + b


def _mods_kernel(c_ref, w_ref, b_ref, o_ref):
    c = c_ref[...]
    a = c * (1.0 / (1.0 + jnp.exp(-c)))
    o_ref[0] = _dot3(a, w_ref[0]) + b_ref[0]


def _mods(c, ada_w, ada_b):
    B, D = c.shape
    L, _, N = ada_w.shape
    rows = 8
    cp = jnp.zeros((rows, D), F32).at[:B].set(c)
    tn = 1536
    return pl.pallas_call(
        _mods_kernel,
        grid=(L, N // tn),
        in_specs=[
            pl.BlockSpec((rows, D), lambda l, j: (0, 0)),
            pl.BlockSpec((1, D, tn), lambda l, j: (l, 0, j)),
            pl.BlockSpec((1, 1, tn), lambda l, j: (l, 0, j)),
        ],
        out_specs=pl.BlockSpec((1, rows, tn), lambda l, j: (l, 0, j)),
        out_shape=jax.ShapeDtypeStruct((L, rows, N), F32),
        compiler_params=_cparams("parallel", "parallel"),
        name="adaln_mods",
    )(cp, ada_w, ada_b.reshape(L, 1, N))


def _mla_proj_kernel(x_ref, mod_ref, cos_ref, sin_ref, win_ref, qn_ref, kvn_ref, wq_ref, wkv_ref,
                     q_ref, kn_ref, kpe_ref, v_ref, *, scale):
    m = mod_ref[0]
    half = MLA_ROPE // 2
    hw = MLA_HEADS * LANES
    for r0 in range(0, x_ref.shape[0], PROJ_SUB_ROWS):
        rows = slice(r0, r0 + PROJ_SUB_ROWS)
        h = (x_ref[rows, :] * (1.0 + m[1:2]) + m[0:1]).astype(BF16)
        z = _dot(h, win_ref[...])
        cq = z[:, :MLA_Q_LORA]
        ckv = z[:, MLA_Q_LORA:MLA_Q_LORA + MLA_KV_LORA]
        kr = z[:, MLA_Q_LORA + MLA_KV_LORA:]
        lane = lax.broadcasted_iota(jnp.int32, kr.shape, 1)
        cos = cos_ref[rows, :]
        sin = sin_ref[rows, :]
        sin_up = jnp.where(lane < half, -sin, 0.0)
        sin_dn = jnp.where((lane >= half) & (lane < MLA_ROPE), sin, 0.0)

        def rope(t):
            return t * cos + pltpu.roll(t, LANES - half, 1) * sin_up + pltpu.roll(t, half, 1) * sin_dn

        qn = (cq * lax.rsqrt(jnp.mean(cq * cq, axis=-1, keepdims=True) + LATENT_EPS) * qn_ref[...]).astype(BF16)
        kvn = (ckv * lax.rsqrt(jnp.mean(ckv * ckv, axis=-1, keepdims=True) + LATENT_EPS)
               * kvn_ref[...]).astype(BF16)
        q = _dot(qn, wq_ref[...])
        kv = _dot(kvn, wkv_ref[...])
        kpe_ref[0, rows, :] = rope(kr).astype(BF16)
        for hh in range(MLA_HEADS):
            a, b = hh * LANES, (hh + 1) * LANES
            q_ref[0, hh, rows, 0:LANES] = (q[:, a:b] * scale).astype(BF16)
            q_ref[0, hh, rows, LANES:2 * LANES] = (rope(q[:, hw + a:hw + b]) * scale).astype(BF16)
            kn_ref[0, hh, rows, :] = kv[:, a:b].astype(BF16)
            v_ref[0, hh, rows, :] = kv[:, hw + a:hw + b].astype(BF16)


def _mla_proj(x2, mods3, cos_t, sin_t, w_in, q_norm, kv_norm, w_uq, w_ukv, B, S):
    T, D = x2.shape
    H = MLA_HEADS
    tm = min(TOKEN_TILE, S)
    npb = S // tm
    win_p = jnp.pad(w_in, ((0, 0), (0, LANES - MLA_ROPE))).astype(BF16)
    wq3 = w_uq.reshape(MLA_Q_LORA, H, MLA_NOPE + MLA_ROPE)
    rope_p = jnp.pad(wq3[:, :, MLA_NOPE:], ((0, 0), (0, 0), (0, LANES - MLA_ROPE))).reshape(MLA_Q_LORA, H * LANES)
    wq_p = jnp.concatenate([wq3[:, :, :MLA_NOPE].reshape(MLA_Q_LORA, H * MLA_NOPE), rope_p],
                           axis=1).astype(BF16)
    wkv3 = w_ukv.reshape(MLA_KV_LORA, H, MLA_NOPE + MLA_V)
    wkv_p = jnp.concatenate([wkv3[:, :, :MLA_NOPE].reshape(MLA_KV_LORA, H * MLA_NOPE),
                             wkv3[:, :, MLA_NOPE:].reshape(MLA_KV_LORA, H * MLA_V)], axis=1).astype(BF16)
    scale = (MLA_NOPE + MLA_ROPE) ** -0.5 * LOG2E
    full = lambda shape: pl.BlockSpec(shape, lambda i: (0,) * len(shape))
    return pl.pallas_call(
        functools.partial(_mla_proj_kernel, scale=scale),
        grid=(T // tm,),
        in_specs=[
            pl.BlockSpec((tm, D), lambda i: (i, 0)),
            pl.BlockSpec((1, 6, D), lambda i: (i // npb, 0, 0)),
            pl.BlockSpec((tm, LANES), lambda i: (i, 0)),
            pl.BlockSpec((tm, LANES), lambda i: (i, 0)),
            full(win_p.shape),
            full((1, MLA_Q_LORA)),
            full((1, MLA_KV_LORA)),
            full(wq_p.shape),
            full(wkv_p.shape),
        ],
        out_specs=[
            pl.BlockSpec((1, H, tm, 2 * LANES), lambda i: (i // npb, 0, i % npb, 0)),
            pl.BlockSpec((1, H, tm, MLA_NOPE), lambda i: (i // npb, 0, i % npb, 0)),
            pl.BlockSpec((1, tm, LANES), lambda i: (i // npb, i % npb, 0)),
            pl.BlockSpec((1, H, tm, MLA_V), lambda i: (i // npb, 0, i % npb, 0)),
        ],
        out_shape=[
            jax.ShapeDtypeStruct((B, H, S, 2 * LANES), BF16),
            jax.ShapeDtypeStruct((B, H, S, MLA_NOPE), BF16),
            jax.ShapeDtypeStruct((B, S, LANES), BF16),
            jax.ShapeDtypeStruct((B, H, S, MLA_V), BF16),
        ],
        compiler_params=_cparams("parallel"),
        name="mla_proj",
    )(x2, mods3, cos_t, sin_t, win_p, q_norm.reshape(1, -1), kv_norm.reshape(1, -1), wq_p, wkv_p)


_NT = (((1,), (1,)), ((), ()))


def _attn_pipeline(load_q, n_maps, load_k, vt_ref, st_ref, m_ref, emit, n_blocks):
    S = vt_ref.shape[1]
    tk = min(KEY_CHUNK, S)
    chunks = [slice(c * tk, (c + 1) * tk) for c in range(S // tk)]

    def scores(blk, slot):
        for mp, qb in enumerate(load_q(blk)):
            m = None
            for ck in chunks:
                sj = lax.dot_general(load_k(ck), qb, _NT, preferred_element_type=F32)
                st_ref[slot, mp, ck, :] = sj
                mj = jnp.max(sj, axis=0, keepdims=True)
                m = mj if m is None else jnp.maximum(m, mj)
            m_ref[slot, mp] = m

    def attend(blk, slot):
        res = []
        for mp in range(n_maps):
            m = m_ref[slot, mp]
            l = None
            acc = None
            for ck in chunks:
                e = jnp.exp2(st_ref[slot, mp, ck, :] - m)
                lj = jnp.sum(e, axis=0, keepdims=True)
                l = lj if l is None else l + lj
                pv = _dot(vt_ref[:, ck], e.astype(BF16))
                acc = pv if acc is None else acc + pv
            res.append((acc, l))
        emit(blk, res)

    scores(0, 0)

    def body(jj, carry):
        j = 2 * jj
        for u in range(2):
            scores(jnp.minimum(j + u + 1, n_blocks - 1), (u + 1) % 2)
            attend(j + u, u % 2)
        return carry

    lax.fori_loop(0, n_blocks // 2, body, 0)


def _block_rows(blk):
    return pl.ds(pl.multiple_of(blk * ROW_BLOCK, ROW_BLOCK), ROW_BLOCK)


def _mla_attn_kernel(q_ref, kn_ref, kpe_ref, v_ref, o_ref, vt_ref, st_ref, m_ref):
    vt_ref[...] = v_ref[0, 0].astype(F32).T.astype(BF16)

    def load_k(ck):
        return jnp.concatenate([kn_ref[0, 0, ck, :], kpe_ref[0, ck, :]], axis=1)

    def emit(blk, res):
        (acc, l), = res
        o_ref[0, _block_rows(blk), :] = (acc * (1.0 / l)).T.astype(BF16)

    _attn_pipeline(lambda blk: [q_ref[0, 0, _block_rows(blk), :]], 1, load_k, vt_ref, st_ref, m_ref, emit,
                   q_ref.shape[2] // ROW_BLOCK)


def _diff_attn_kernel(lam_ref, subln_ref, q_ref, k_ref, v_ref, o_ref, vt_ref, st_ref, m_ref, *, lambda_init):
    vt_ref[...] = v_ref[0, 0].astype(F32).T.astype(BF16)
    lf = lam_ref[...]
    lam = (jnp.exp(jnp.sum(lf[0:1] * lf[1:2], axis=-1, keepdims=True))
           - jnp.exp(jnp.sum(lf[2:3] * lf[3:4], axis=-1, keepdims=True)) + lambda_init)

    def load_q(blk):
        q = q_ref[0, 0, _block_rows(blk), :]
        lane = lax.broadcasted_iota(jnp.int32, q.shape, 1)
        zero = jnp.zeros_like(q)
        return [jnp.where(lane < DIFF_HEAD_DIM, q, zero), jnp.where(lane >= DIFF_HEAD_DIM, q, zero)]

    def emit(blk, res):
        (acc1, l1), (acc2, l2) = res
        o = acc1 * (1.0 / l1) - acc2 * (lam / l2)
        o = o * lax.rsqrt(jnp.mean(o * o, axis=0, keepdims=True) + NORM_EPS) * subln_ref[...]
        o_ref[0, _block_rows(blk), :] = (o * (1.0 - lambda_init)).T.astype(BF16)

    _attn_pipeline(load_q, 2, lambda ck: k_ref[0, 0, ck, :], vt_ref, st_ref, m_ref, emit,
                   q_ref.shape[2] // ROW_BLOCK)


def _attention(kernel_fn, extra, n_maps, q, keys, v, name):
    B, H, S, dq = q.shape
    dv = v.shape[-1]
    assert (S // ROW_BLOCK) % 2 == 0
    extra_specs = [pl.BlockSpec(a.shape, lambda b, h, n=a.ndim: (0,) * n) for a in extra]
    key_specs = [pl.BlockSpec((1, 1, S, kp.shape[-1]), lambda b, h: (b, h, 0, 0)) if kp.ndim == 4
                 else pl.BlockSpec((1, S, kp.shape[-1]), lambda b, h: (b, 0, 0)) for kp in keys]
    return pl.pallas_call(
        kernel_fn,
        grid=(B, H),
        in_specs=extra_specs + [pl.BlockSpec((1, 1, S, dq), lambda b, h: (b, h, 0, 0))] + key_specs + [
            pl.BlockSpec((1, 1, S, dv), lambda b, h: (b, h, 0, 0)),
        ],
        out_specs=pl.BlockSpec((1, S, dv), lambda b, h: (b, 0, h)),
        out_shape=jax.ShapeDtypeStruct((B, S, H * dv), BF16),
        scratch_shapes=[
            pltpu.VMEM((dv, S), BF16),
            pltpu.VMEM((2, n_maps, S, ROW_BLOCK), F32),
            pltpu.VMEM((2, n_maps, 1, ROW_BLOCK), F32),
        ],
        compiler_params=_cparams("parallel", "parallel"),
        name=name,
    )(*extra, q, *keys, v)


def _diff_proj_kernel(x_ref, mod_ref, cos_ref, sin_ref, win_ref, q_ref, k_ref, v_ref, *, scale):
    m = mod_ref[0]
    h = (x_ref[...] * (1.0 + m[1:2]) + m[0:1]).astype(BF16)
    z = _dot(h, win_ref[...])
    cos_t = cos_ref[...]
    sin_t = sin_ref[...]
    lane = lax.broadcasted_iota(jnp.int32, cos_t.shape, 1)
    low = lane < DIFF_HEAD_DIM
    r = lane % DIFF_HEAD_DIM
    hr = DIFF_ROT // 2
    c = jnp.where(r < DIFF_ROT, jnp.where(low, pltpu.roll(cos_t, DIFF_HEAD_DIM, 1), cos_t), 1.0)
    s = jnp.where(low, pltpu.roll(sin_t, DIFF_HEAD_DIM, 1), sin_t)
    sa = jnp.where(r < hr, -s, 0.0)
    sb = jnp.where((r >= hr) & (r < DIFF_ROT), s, 0.0)
    hw = DIFF_HEADS * LANES

    def rope(t):
        return t * c + pltpu.roll(t, LANES - DIFF_ROT // 2, 1) * sa + pltpu.roll(t, DIFF_ROT // 2, 1) * sb

    for hh in range(DIFF_HEADS):
        a, b = hh * LANES, (hh + 1) * LANES
        q_ref[0, hh] = (rope(z[:, a:b]) * scale).astype(BF16)
        k_ref[0, hh] = rope(z[:, hw + a:hw + b]).astype(BF16)
        v_ref[0, hh] = z[:, 2 * hw + a:2 * hw + b].astype(BF16)


def _diff_proj(x2, mods3, cos_t, sin_t, w_in, B, S):
    T, D = x2.shape
    H = DIFF_HEADS
    tm = min(TOKEN_TILE, S)
    npb = S // tm
    win = w_in.astype(BF16)
    hd = jax.ShapeDtypeStruct((B, H, S, LANES), BF16)
    hspec = pl.BlockSpec((1, H, tm, LANES), lambda i: (i // npb, 0, i % npb, 0))
    return pl.pallas_call(
        functools.partial(_diff_proj_kernel, scale=DIFF_HEAD_DIM ** -0.5 * LOG2E),
        grid=(T // tm,),
        in_specs=[
            pl.BlockSpec((tm, D), lambda i: (i, 0)),
            pl.BlockSpec((1, 6, D), lambda i: (i // npb, 0, 0)),
            pl.BlockSpec((tm, LANES), lambda i: (i, 0)),
            pl.BlockSpec((tm, LANES), lambda i: (i, 0)),
            pl.BlockSpec(win.shape, lambda i: (0, 0)),
        ],
        out_specs=[hspec, hspec, hspec],
        out_shape=[hd, hd, hd],
        compiler_params=_cparams("parallel"),
        name="diff_proj",
    )(x2, mods3, cos_t, sin_t, win)


def _post_attn_kernel(o_ref, x_ref, mod_ref, wo_ref, g_ref, b_ref, rw_ref, x1_ref, aff_ref):
    m = mod_ref[0]
    w_hi, w_lo = _split_bf16(rw_ref[...])
    rw = jnp.concatenate([w_hi, w_lo], axis=1)
    wo = wo_ref[...].astype(BF16)
    for r0 in range(0, o_ref.shape[0], POST_SUB_ROWS):
        rows = slice(r0, r0 + POST_SUB_ROWS)
        t = _dot(o_ref[rows, :], wo)
        x1 = _layer_norm(ALPHA * x_ref[rows, :] + (1.0 + m[2:3]) * t, g_ref[...], b_ref[...])
        x1_ref[rows, :] = x1
        h2 = x1 * (1.0 + m[4:5]) + m[3:4]
        h_hi, h_lo = _split_bf16(h2)
        hw = _dot(h_hi, rw)
        logits = hw[:, :LANES] + (hw[:, LANES:] + _dot(h_lo, w_hi))
        lt = logits.T[:N_EXPERTS]
        mx = jnp.max(lt, axis=0, keepdims=True)
        ex = jnp.exp(lt - mx)
        aff_ref[0, :, rows] = ex / jnp.sum(ex, axis=0, keepdims=True)


def _post_attn(o2, x2, mods3, w_o, ln_g, ln_b, router_w, B, S):
    T, D = x2.shape
    E = N_EXPERTS
    tm = min(POST_TILE, S)
    npb = S // tm
    rw = jnp.pad(router_w, ((0, 0), (0, LANES - E)))
    return pl.pallas_call(
        _post_attn_kernel,
        grid=(T // tm,),
        in_specs=[
            pl.BlockSpec((tm, D), lambda i: (i, 0)),
            pl.BlockSpec((tm, D), lambda i: (i, 0)),
            pl.BlockSpec((1, 6, D), lambda i: (i // npb, 0, 0)),
            pl.BlockSpec((D, D), lambda i: (0, 0)),
            pl.BlockSpec((1, D), lambda i: (0, 0)),
            pl.BlockSpec((1, D), lambda i: (0, 0)),
            pl.BlockSpec((D, LANES), lambda i: (0, 0)),
        ],
        out_specs=[
            pl.BlockSpec((tm, D), lambda i: (i, 0)),
            pl.BlockSpec((1, E, tm), lambda i: (i // npb, 0, i % npb)),
        ],
        out_shape=[
            jax.ShapeDtypeStruct((T, D), F32),
            jax.ShapeDtypeStruct((B, E, S), F32),
        ],
        compiler_params=_cparams("parallel"),
        name="post_attn",
    )(o2, x2, mods3, w_o, ln_g.reshape(1, D), ln_b.reshape(1, D), rw)


def _topk_kernel(aff_ref, idx_ref, gate_ref, pos_ref, off_ref, *, cap):
    a3 = aff_ref[0]
    E, NC, _ = a3.shape
    R = E * NC

    def count3(mask3):
        c = jnp.sum(jnp.where(mask3, 1.0, 0.0), axis=1, keepdims=True)
        return jnp.sum(c, axis=2, keepdims=True)

    def body(i, bits):
        cand = bits | jnp.left_shift(jnp.int32(1), 30 - i)
        reach = count3(a3 >= lax.bitcast_convert_type(cand, F32))
        return jnp.where(reach >= cap, cand, bits)

    thr = lax.bitcast_convert_type(lax.fori_loop(0, 31, body, jnp.zeros((E, 1, 1), jnp.int32)), F32)
    gt3 = a3 > thr
    eq3 = a3 == thr
    need = cap - count3(gt3)

    li = lax.broadcasted_iota(jnp.int32, (LANES, LANES), 0)
    lj = lax.broadcasted_iota(jnp.int32, (LANES, LANES), 1)
    tri_excl = jnp.where(li < lj, 1.0, 0.0).astype(BF16)
    tri_incl = jnp.where(li <= lj, 1.0, 0.0).astype(BF16)
    ri = lax.broadcasted_iota(jnp.int32, (R, R), 0)
    rj = lax.broadcasted_iota(jnp.int32, (R, R), 1)
    same = (ri // NC) == (rj // NC)
    blk_lower = jnp.where(same & (rj < ri), 1.0, 0.0).astype(BF16)

    def chunk_offsets(mask2):
        tot = jnp.sum(mask2, axis=1, keepdims=True)
        totb = jnp.broadcast_to(tot, (R, LANES)).astype(BF16)
        return _dot(blk_lower, totb), tot

    eq2 = jnp.where(eq3, 1.0, 0.0).reshape(R, LANES)
    eq_off, _ = chunk_offsets(eq2)
    eq_rank = (eq_off + _dot(eq2.astype(BF16), tri_excl)).reshape(E, NC, LANES)
    sel3 = gt3 | (eq3 & (eq_rank < need))
    sel2 = jnp.where(sel3, 1.0, 0.0).reshape(R, LANES)
    off2, tot2 = chunk_offsets(sel2)
    cum2 = _dot(sel2.astype(BF16), tri_incl)
    pos_ref[0] = jnp.where(sel3, (off2 + cum2 - 1.0).reshape(E, NC, LANES), -1.0).astype(jnp.int32)
    off_ref[0] = off2.reshape(E, NC, LANES).astype(jnp.int32)
    a2 = a3.reshape(R, LANES)
    a_hi = a2.astype(BF16)
    a_mid = (a2 - a_hi.astype(F32)).astype(BF16)
    a_lo = (a2 - a_hi.astype(F32) - a_mid.astype(F32)).astype(BF16)

    cs = lax.broadcasted_iota(jnp.int32, (NC, cap), 1).astype(F32)
    ks = lax.broadcasted_iota(jnp.int32, (NC, cap), 0).astype(F32)
    ls = lax.broadcasted_iota(jnp.int32, (LANES, cap), 0).astype(F32)
    for e in range(E):
        rows = slice(e * NC, (e + 1) * NC)
        off_e = off2[rows, 0:1]
        end_e = off_e + tot2[rows]
        k_of_c = jnp.sum(jnp.where(end_e <= cs, 1.0, 0.0), axis=0, keepdims=True)
        onehot = ks == k_of_c
        oh = jnp.where(onehot, 1.0, 0.0).astype(BF16)
        off_c = jnp.sum(jnp.where(onehot, off_e, 0.0), axis=0, keepdims=True)
        rank_c = cs[0:1] - off_c
        cum_c = _dot(cum2[rows].T.astype(BF16), oh)
        lane_c = jnp.sum(jnp.where(cum_c <= rank_c, 1.0, 0.0), axis=0, keepdims=True)
        idx_ref[0, e] = (k_of_c * LANES + lane_c).astype(jnp.int32)
        g_c = (_dot(a_hi[rows].T, oh) + _dot(a_mid[rows].T, oh)) + _dot(a_lo[rows].T, oh)
        gate_ref[0, e] = jnp.sum(jnp.where(ls == lane_c, g_c, 0.0), axis=0, keepdims=True)


def _topk(aff, cap):
    B, E, S = aff.shape
    NC = S // LANES
    slot_spec = pl.BlockSpec((1, E, 1, cap), lambda b: (b, 0, 0, 0))
    tok_spec = pl.BlockSpec((1, E, NC, LANES), lambda b: (b, 0, 0, 0))
    return pl.pallas_call(
        functools.partial(_topk_kernel, cap=cap),
        grid=(B,),
        in_specs=[tok_spec],
        out_specs=[slot_spec, slot_spec, tok_spec, tok_spec],
        out_shape=[
            jax.ShapeDtypeStruct((B, E, 1, cap), jnp.int32),
            jax.ShapeDtypeStruct((B, E, 1, cap), F32),
            jax.ShapeDtypeStruct((B, E, NC, LANES), jnp.int32),
            jax.ShapeDtypeStruct((B, E, NC, LANES), jnp.int32),
        ],
        compiler_params=_cparams("parallel"),
        name="ec_topk",
    )(aff.reshape(B, E, NC, LANES))


def _dispatch_kernel(idx_ref, x_ref, mod_ref, xe_ref, rows_ref):
    cap = xe_ref.shape[1]
    m = mod_ref[0]
    scale = 1.0 + m[4:5]
    shift = m[3:4]

    def group(g, carry):
        for r in range(SLOT_ALIGN):
            rows_ref[r:r + 1, :] = x_ref[pl.ds(idx_ref[0, 0, g * SLOT_ALIGN + r], 1), :]
        start = pl.multiple_of(g * SLOT_ALIGN, SLOT_ALIGN)
        xe_ref[0, pl.ds(start, SLOT_ALIGN), :] = (rows_ref[...] * scale + shift).astype(BF16)
        return carry

    lax.fori_loop(0, cap // SLOT_ALIGN, group, 0)


def _dispatch(idx, x1, mods3, B, S, cap):
    E = idx.shape[1]
    D = x1.shape[1]
    return pl.pallas_call(
        _dispatch_kernel,
        grid=(B, E),
        in_specs=[
            pl.BlockSpec((1, 1, cap), lambda b, e: (b * E + e, 0, 0), memory_space=pltpu.SMEM),
            pl.BlockSpec((S, D), lambda b, e: (b, 0)),
            pl.BlockSpec((1, 6, D), lambda b, e: (b, 0, 0)),
        ],
        out_specs=pl.BlockSpec((1, cap, D), lambda b, e: (e, b, 0)),
        out_shape=jax.ShapeDtypeStruct((E, B * cap, D), BF16),
        scratch_shapes=[pltpu.VMEM((SLOT_ALIGN, D), F32)],
        compiler_params=_cparams("parallel", "parallel"),
        name="moe_dispatch",
    )(idx.reshape(B * E, 1, cap), x1, mods3)


def _moe_ffn_kernel(xe_ref, gate_ref, wg_ref, wu_ref, wo_ref, y_ref, act_ref, *, n_hidden_steps):
    s = pl.program_id(1)
    tf = wg_ref.shape[3]

    for f in range(n_hidden_steps):
        @pl.when(s == f)
        def _():
            wg = wg_ref[0, 0].astype(BF16)
            wu = wu_ref[0, 0].astype(BF16)
            for r0 in range(0, xe_ref.shape[1], FFN_SUB_ROWS):
                rows = slice(r0, r0 + FFN_SUB_ROWS)
                xe = xe_ref[0, rows, :]
                g = _dot(xe, wg)
                u = _dot(xe, wu)
                act_ref[rows, f * tf:(f + 1) * tf] = (g * (1.0 / (1.0 + jnp.exp(-g))) * u).astype(BF16)

    @pl.when(s >= n_hidden_steps)
    def _():
        out = _dot(act_ref[...], wo_ref[0, 0].astype(BF16))
        diag = (lax.broadcasted_iota(jnp.int32, (LANES, LANES), 0)
                == lax.broadcasted_iota(jnp.int32, (LANES, LANES), 1))
        for r0 in range(0, out.shape[0], LANES):
            col = jnp.sum(jnp.where(diag, gate_ref[0, :, r0:r0 + LANES], 0.0), axis=1, keepdims=True)
            y_ref[0, r0:r0 + LANES, :] = (out[r0:r0 + LANES, :] * col).astype(BF16)


def _moe_ffn(xe, gate_row, w_in, w_out, layer):
    E, M, D = xe.shape
    FF = w_out.shape[2]
    tf = FF_TILE
    nf = FF // tf
    tn = FFN_OUT_TILE
    nn = D // tn
    hid = lambda s: jnp.minimum(s, nf - 1)
    col = lambda s: jnp.maximum(s - nf, 0)
    return pl.pallas_call(
        functools.partial(_moe_ffn_kernel, n_hidden_steps=nf),
        grid=(E, nf + nn),
        in_specs=[
            pl.BlockSpec((1, M, D), lambda e, s: (e, 0, 0)),
            pl.BlockSpec((1, 1, M), lambda e, s: (e, 0, 0)),
            pl.BlockSpec((1, 1, D, tf), lambda e, s: (layer, e, 0, hid(s))),
            pl.BlockSpec((1, 1, D, tf), lambda e, s: (layer, e, 0, nf + hid(s))),
            pl.BlockSpec((1, 1, FF, tn), lambda e, s: (layer, e, 0, col(s))),
        ],
        out_specs=pl.BlockSpec((1, M, tn), lambda e, s: (e, 0, col(s))),
        out_shape=jax.ShapeDtypeStruct((E, M, D), BF16),
        scratch_shapes=[pltpu.VMEM((M, FF), BF16)],
        compiler_params=_cparams("parallel", "arbitrary"),
        name="moe_ffn",
    )(xe, gate_row, w_in, w_in, w_out)


def _combine_ln_kernel(win_ref, pos_ref, y_ref, x_ref, mod_ref, g_ref, b_ref, o_ref, *, window):
    b = pl.program_id(0)
    j = pl.program_id(1)
    n_exp = y_ref.shape[0]
    tiles = pos_ref.shape[1] // LANES
    slot = lax.broadcasted_iota(jnp.int32, (LANES, window), 1)
    m = mod_ref[0]
    for t in range(tiles):
        rows = slice(t * LANES, (t + 1) * LANES)
        pos = pos_ref[0, rows, :]
        tile = (b * pl.num_programs(1) + j) * tiles + t
        f = None
        for e in range(n_exp):
            start = pl.multiple_of(win_ref[tile * n_exp + e], SLOT_ALIGN)
            onehot = jnp.where(pos[:, e:e + 1] - start == slot, 1.0, 0.0).astype(BF16)
            part = _dot(onehot, y_ref[e, pl.ds(start, window), :])
            f = part if f is None else f + part
        o_ref[rows, :] = _layer_norm(ALPHA * x_ref[rows, :] + (1.0 + m[5:6]) * f, g_ref[...], b_ref[...])


def _combine_ln(pos_t, win, y, x1, mods3, ln_g, ln_b, B, S, cap, window):
    E, _, D = y.shape
    tm = min(COMBINE_TILE, S)
    nj = S // tm
    row = pl.BlockSpec((tm, D), lambda b, j, w: (b * nj + j, 0))
    vec = pl.BlockSpec((1, D), lambda b, j, w: (0, 0))
    return pl.pallas_call(
        functools.partial(_combine_ln_kernel, window=window),
        grid_spec=pltpu.PrefetchScalarGridSpec(
            num_scalar_prefetch=1,
            grid=(B, nj),
            in_specs=[
                pl.BlockSpec((1, tm, E), lambda b, j, w: (b, j, 0)),
                pl.BlockSpec((E, cap, D), lambda b, j, w: (0, b, 0)),
                row,
                pl.BlockSpec((1, 6, D), lambda b, j, w: (b, 0, 0)),
                vec,
                vec,
            ],
            out_specs=row,
        ),
        out_shape=jax.ShapeDtypeStruct((B * S, D), F32),
        compiler_params=_cparams("parallel", "arbitrary"),
        name="moe_combine_ln",
    )(win, pos_t, y, x1, mods3, ln_g.reshape(1, D), ln_b.reshape(1, D))


def _rope_tables(positions):
    B, S = positions.shape
    pos = positions.astype(F32).reshape(B * S, 1)
    inv_m = MLA_THETA ** (-jnp.arange(0, MLA_ROPE, 2, dtype=F32) / MLA_ROPE)
    inv_d = ROPE_THETA ** (-jnp.arange(0, DIFF_ROT, 2, dtype=F32) / DIFF_ROT)
    rest = jnp.zeros((LANES - MLA_ROPE - DIFF_ROT,), F32)
    ang = pos * jnp.concatenate([inv_m, inv_m, inv_d, inv_d, rest])
    return jnp.cos(ang), jnp.sin(ang)


def _moe_block(x1, aff, mods3, w_in, w_out, layer, ln_g, ln_b, B, S):
    E = N_EXPERTS
    cap = max(1, EC_CAPACITY_FACTOR * S // E)
    idx, gate, pos, off = _topk(aff, cap)
    gate_row = gate[:, :, 0, :].transpose(1, 0, 2).reshape(E, 1, B * cap)
    xe = _dispatch(idx, x1, mods3, B, S, cap)
    y = _moe_ffn(xe, gate_row, w_in, w_out, layer)
    window = min(COMBINE_WINDOW, cap)
    pos_t = pos.reshape(B, E, S).transpose(0, 2, 1)
    win = jnp.minimum(off[:, :, :, 0] // SLOT_ALIGN * SLOT_ALIGN, cap - window)
    win = win.transpose(0, 2, 1).reshape(-1)
    return _combine_ln(pos_t, win, y, x1, mods3, ln_g, ln_b, B, S, cap, window)


def kernel(x, c, positions, ada_w, ada_b, ln1_g, ln1_b, ln2_g, ln2_b, mla_w_in, mla_q_norm, mla_kv_norm,
           mla_w_uq, mla_w_ukv, mla_w_o, diff_w_in, diff_lambda, diff_subln, diff_w_o, router_w, moe_w_in,
           moe_w_out):
    B, S, D = x.shape
    cos_t, sin_t = _rope_tables(positions)
    mods = _mods(c, ada_w, ada_b)
    x2 = x.reshape(B * S, D)
    for i in range(DEPTH):
        mods3 = mods[i, :B].reshape(B, 6, D)
        j = i // 2
        if i % 2 == 0:
            q, kn, kpe, v = _mla_proj(x2, mods3, cos_t, sin_t, mla_w_in[j], mla_q_norm[j], mla_kv_norm[j],
                                      mla_w_uq[j], mla_w_ukv[j], B, S)
            o = _attention(_mla_attn_kernel, [], 1, q, [kn, kpe], v, "mla_attn")
            w_o = mla_w_o[j]
        else:
            lambda_init = 0.8 - 0.6 * math.exp(-0.3 * i)
            q, k, v = _diff_proj(x2, mods3, cos_t, sin_t, diff_w_in[j], B, S)
            o = _attention(functools.partial(_diff_attn_kernel, lambda_init=lambda_init),
                           [diff_lambda[j], diff_subln[j].reshape(DIFF_V, 1)], 2, q, [k], v, "diff_attn")
            w_o = diff_w_o[j]
        x1, aff = _post_attn(o.reshape(B * S, D), x2, mods3, w_o, ln1_g[i], ln1_b[i], router_w[i], B, S)
        x2 = _moe_block(x1, aff, mods3, moe_w_in, moe_w_out, i, ln2_g[i], ln2_b[i], B, S)
    return x2.reshape(B, S, D)
```

```python
import functools
import math

import jax
import jax.numpy as jnp
from jax import lax
from jax.experimental import pallas as pl
from jax.experimental.pallas import tpu as pltpu

F32 = jnp.float32
BF16 = jnp.bfloat16

D_MODEL = 1024
DEPTH = 2
MLA_HEADS = 8
MLA_Q_LORA = 384
MLA_KV_LORA = 256
MLA_NOPE = 128
MLA_ROPE = 64
MLA_V = 128
MLA_THETA = 10000.0
DIFF_HEADS = 8
DIFF_HEAD_DIM = 64
DIFF_V = 2 * DIFF_HEAD_DIM
DIFF_ROT = DIFF_HEAD_DIM // 4
ROPE_THETA = 500000.0
N_EXPERTS = 16
EXPERT_FF = 2048
EC_CAPACITY_FACTOR = 2
NORM_EPS = 1e-5
LATENT_EPS = 1e-6
ALPHA = (2 * DEPTH) ** 0.25

LANES = 128
VMEM_LIMIT = 56 * 1024 * 1024

TOKEN_TILE = 1024
PROJ_SUB_ROWS = 256
POST_TILE = 1024
POST_SUB_ROWS = 256
KEY_CHUNK = 512
ROW_BLOCK = 256
HEADS_PER_STEP = 2
LOG2E = math.log2(math.e)
FF_TILE = 512
FFN_OUT_TILE = 512
FFN_SUB_ROWS = 512
SLOT_ALIGN = 16
COMBINE_WINDOW = 256
COMBINE_TILE = 512


def _cparams(*sem):
    return pltpu.CompilerParams(dimension_semantics=sem, vmem_limit_bytes=VMEM_LIMIT)


def _dot(a, b):
    return jnp.dot(a, b, preferred_element_type=F32)


def _split_bf16(a):
    hi = a.astype(BF16)
    lo = (a - hi.astype(F32)).astype(BF16)
    return hi, lo


def _dot3(a, b):
    ah, al = _split_bf16(a)
    bh, bl = _split_bf16(b)
    return _dot(ah, bh) + (_dot(ah, bl) + _dot(al, bh))


def _layer_norm(y, g, b):
    mu = jnp.mean(y, axis=-1, keepdims=True)
    yc = y - mu
    var = jnp.mean(yc * yc, axis=-1, keepdims=True)
    return yc * lax.rsqrt(var + NORM_EPS) * g + b


def _mods_kernel(c_ref, w_ref, b_ref, o_ref):
    c = c_ref[...]
    a = c * (1.0 / (1.0 + jnp.exp(-c)))
    o_ref[0] = _dot3(a, w_ref[0]) + b_ref[0]


def _mods(c, ada_w, ada_b):
    B, D = c.shape
    L, _, N = ada_w.shape
    rows = 8
    cp = jnp.zeros((rows, D), F32).at[:B].set(c)
    tn = 1536
    return pl.pallas_call(
        _mods_kernel,
        grid=(L, N // tn),
        in_specs=[
            pl.BlockSpec((rows, D), lambda l, j: (0, 0)),
            pl.BlockSpec((1, D, tn), lambda l, j: (l, 0, j)),
            pl.BlockSpec((1, 1, tn), lambda l, j: (l, 0, j)),
        ],
        out_specs=pl.BlockSpec((1, rows, tn), lambda l, j: (l, 0, j)),
        out_shape=jax.ShapeDtypeStruct((L, rows, N), F32),
        compiler_params=_cparams("parallel", "parallel"),
        name="adaln_mods",
    )(cp, ada_w, ada_b.reshape(L, 1, N))


def _mla_proj_kernel(x_ref, mod_ref, cos_ref, sin_ref, win_ref, qn_ref, kvn_ref, wq_ref, wkv_ref,
                     q_ref, kn_ref, kpe_ref, v_ref, *, scale):
    m = mod_ref[0]
    half = MLA_ROPE // 2
    hw = MLA_HEADS * LANES
    for r0 in range(0, x_ref.shape[0], PROJ_SUB_ROWS):
        rows = slice(r0, r0 + PROJ_SUB_ROWS)
        h = (x_ref[rows, :] * (1.0 + m[1:2]) + m[0:1]).astype(BF16)
        z = _dot(h, win_ref[...])
        cq = z[:, :MLA_Q_LORA]
        ckv = z[:, MLA_Q_LORA:MLA_Q_LORA + MLA_KV_LORA]
        kr = z[:, MLA_Q_LORA + MLA_KV_LORA:]
        lane = lax.broadcasted_iota(jnp.int32, kr.shape, 1)
        cos = cos_ref[rows, :]
        sin = sin_ref[rows, :]
        sin_up = jnp.where(lane < half, -sin, 0.0)
        sin_dn = jnp.where((lane >= half) & (lane < MLA_ROPE), sin, 0.0)

        def rope(t):
            return t * cos + pltpu.roll(t, LANES - half, 1) * sin_up + pltpu.roll(t, half, 1) * sin_dn

        qn = (cq * lax.rsqrt(jnp.mean(cq * cq, axis=-1, keepdims=True) + LATENT_EPS) * qn_ref[...]).astype(BF16)
        kvn = (ckv * lax.rsqrt(jnp.mean(ckv * ckv, axis=-1, keepdims=True) + LATENT_EPS)
               * kvn_ref[...]).astype(BF16)
        q = _dot(qn, wq_ref[...])
        kv = _dot(kvn, wkv_ref[...])
        kpe_ref[0, rows, :] = rope(kr).astype(BF16)
        for hh in range(MLA_HEADS):
            a, b = hh * LANES, (hh + 1) * LANES
            q_ref[0, hh, rows, 0:LANES] = (q[:, a:b] * scale).astype(BF16)
            q_ref[0, hh, rows, LANES:2 * LANES] = (rope(q[:, hw + a:hw + b]) * scale).astype(BF16)
            kn_ref[0, hh, rows, :] = kv[:, a:b].astype(BF16)
            v_ref[0, hh, rows, :] = kv[:, hw + a:hw + b].astype(BF16)


def _mla_proj(x2, mods3, cos_t, sin_t, w_in, q_norm, kv_norm, w_uq, w_ukv, B, S):
    T, D = x2.shape
    H = MLA_HEADS
    tm = min(TOKEN_TILE, S)
    npb = S // tm
    win_p = jnp.pad(w_in, ((0, 0), (0, LANES - MLA_ROPE))).astype(BF16)
    wq3 = w_uq.reshape(MLA_Q_LORA, H, MLA_NOPE + MLA_ROPE)
    rope_p = jnp.pad(wq3[:, :, MLA_NOPE:], ((0, 0), (0, 0), (0, LANES - MLA_ROPE))).reshape(MLA_Q_LORA, H * LANES)
    wq_p = jnp.concatenate([wq3[:, :, :MLA_NOPE].reshape(MLA_Q_LORA, H * MLA_NOPE), rope_p],
                           axis=1).astype(BF16)
    wkv3 = w_ukv.reshape(MLA_KV_LORA, H, MLA_NOPE + MLA_V)
    wkv_p = jnp.concatenate([wkv3[:, :, :MLA_NOPE].reshape(MLA_KV_LORA, H * MLA_NOPE),
                             wkv3[:, :, MLA_NOPE:].reshape(MLA_KV_LORA, H * MLA_V)], axis=1).astype(BF16)
    scale = (MLA_NOPE + MLA_ROPE) ** -0.5 * LOG2E
    full = lambda shape: pl.BlockSpec(shape, lambda i: (0,) * len(shape))
    return pl.pallas_call(
        functools.partial(_mla_proj_kernel, scale=scale),
        grid=(T // tm,),
        in_specs=[
            pl.BlockSpec((tm, D), lambda i: (i, 0)),
            pl.BlockSpec((1, 6, D), lambda i: (i // npb, 0, 0)),
            pl.BlockSpec((tm, LANES), lambda i: (i, 0)),
            pl.BlockSpec((tm, LANES), lambda i: (i, 0)),
            full(win_p.shape),
            full((1, MLA_Q_LORA)),
            full((1, MLA_KV_LORA)),
            full(wq_p.shape),
            full(wkv_p.shape),
        ],
        out_specs=[
            pl.BlockSpec((1, H, tm, 2 * LANES), lambda i: (i // npb, 0, i % npb, 0)),
            pl.BlockSpec((1, H, tm, MLA_NOPE), lambda i: (i // npb, 0, i % npb, 0)),
            pl.BlockSpec((1, tm, LANES), lambda i: (i // npb, i % npb, 0)),
            pl.BlockSpec((1, H, tm, MLA_V), lambda i: (i // npb, 0, i % npb, 0)),
        ],
        out_shape=[
            jax.ShapeDtypeStruct((B, H, S, 2 * LANES), BF16),
            jax.ShapeDtypeStruct((B, H, S, MLA_NOPE), BF16),
            jax.ShapeDtypeStruct((B, S, LANES), BF16),
            jax.ShapeDtypeStruct((B, H, S, MLA_V), BF16),
        ],
        compiler_params=_cparams("parallel"),
        name="mla_proj",
    )(x2, mods3, cos_t, sin_t, win_p, q_norm.reshape(1, -1), kv_norm.reshape(1, -1), wq_p, wkv_p)


_NT = (((1,), (1,)), ((), ()))


def _attn_pipeline(load_q, n_maps, load_k, vt_ref, st_ref, m_ref, emit, n_blocks):
    n_heads, _, S = vt_ref.shape
    tk = min(KEY_CHUNK, S)
    chunks = [slice(c * tk, (c + 1) * tk) for c in range(S // tk)]

    def scores(head, blk, slot):
        for mp, qb in enumerate(load_q(head, blk)):
            m = None
            for ck in chunks:
                sj = lax.dot_general(load_k(head, ck), qb, _NT, preferred_element_type=F32)
                st_ref[slot, mp, ck, :] = sj
                mj = jnp.max(sj, axis=0, keepdims=True)
                m = mj if m is None else jnp.maximum(m, mj)
            m_ref[slot, mp] = m

    def attend(head, blk, slot):
        res = []
        for mp in range(n_maps):
            m = m_ref[slot, mp]
            l = None
            acc = None
            for ck in chunks:
                e = jnp.exp2(st_ref[slot, mp, ck, :] - m)
                lj = jnp.sum(e, axis=0, keepdims=True)
                l = lj if l is None else l + lj
                pv = _dot(vt_ref[head, :, ck], e.astype(BF16))
                acc = pv if acc is None else acc + pv
            res.append((acc, l))
        emit(head, blk, res)

    scores(0, 0, 0)
    for head in range(n_heads):
        def body(jj, carry, head=head):
            j = 2 * jj
            for u in range(2):
                nxt = j + u + 1
                if head == n_heads - 1:
                    scores(head, jnp.minimum(nxt, n_blocks - 1), (u + 1) % 2)
                else:
                    wrap = jnp.where(nxt == n_blocks, 1, 0)
                    scores(head + wrap, nxt - wrap * n_blocks, (u + 1) % 2)
                attend(head, j + u, u % 2)
            return carry

        lax.fori_loop(0, n_blocks // 2, body, 0)


def _block_rows(blk):
    return pl.ds(pl.multiple_of(blk * ROW_BLOCK, ROW_BLOCK), ROW_BLOCK)


def _stage_v_transposed(v_ref, vt_ref):
    for head in range(vt_ref.shape[0]):
        vt_ref[head] = v_ref[0, head].astype(F32).T.astype(BF16)


def _mla_attn_kernel(q_ref, kn_ref, kpe_ref, v_ref, o_ref, vt_ref, st_ref, m_ref):
    _stage_v_transposed(v_ref, vt_ref)
    dv = vt_ref.shape[1]

    def load_k(head, ck):
        return jnp.concatenate([kn_ref[0, head, ck, :], kpe_ref[0, ck, :]], axis=1)

    def emit(head, blk, res):
        (acc, l), = res
        o_ref[0, _block_rows(blk), head * dv:(head + 1) * dv] = (acc * (1.0 / l)).T.astype(BF16)

    _attn_pipeline(lambda head, blk: [q_ref[0, head, _block_rows(blk), :]], 1, load_k, vt_ref, st_ref, m_ref,
                   emit, q_ref.shape[2] // ROW_BLOCK)


def _diff_attn_kernel(lam_ref, subln_ref, q_ref, k_ref, v_ref, o_ref, vt_ref, st_ref, m_ref, *, lambda_init):
    _stage_v_transposed(v_ref, vt_ref)
    dv = vt_ref.shape[1]
    lf = lam_ref[...]
    lam = (jnp.exp(jnp.sum(lf[0:1] * lf[1:2], axis=-1, keepdims=True))
           - jnp.exp(jnp.sum(lf[2:3] * lf[3:4], axis=-1, keepdims=True)) + lambda_init)

    def load_q(head, blk):
        q = q_ref[0, head, _block_rows(blk), :]
        lane = lax.broadcasted_iota(jnp.int32, q.shape, 1)
        zero = jnp.zeros_like(q)
        return [jnp.where(lane < DIFF_HEAD_DIM, q, zero), jnp.where(lane >= DIFF_HEAD_DIM, q, zero)]

    def emit(head, blk, res):
        (acc1, l1), (acc2, l2) = res
        o = acc1 * (1.0 / l1) - acc2 * (lam / l2)
        o = o * lax.rsqrt(jnp.mean(o * o, axis=0, keepdims=True) + NORM_EPS) * subln_ref[...]
        o_ref[0, _block_rows(blk), head * dv:(head + 1) * dv] = (o * (1.0 - lambda_init)).T.astype(BF16)

    _attn_pipeline(load_q, 2, lambda head, ck: k_ref[0, head, ck, :], vt_ref, st_ref, m_ref, emit,
                   q_ref.shape[2] // ROW_BLOCK)


def _attention(kernel_fn, extra, n_maps, q, keys, v, name):
    B, H, S, dq = q.shape
    dv = v.shape[-1]
    hps = HEADS_PER_STEP
    assert (S // ROW_BLOCK) % 2 == 0 and H % hps == 0
    per_head = lambda width: pl.BlockSpec((1, hps, S, width), lambda b, h: (b, h, 0, 0))
    extra_specs = [pl.BlockSpec(a.shape, lambda b, h, n=a.ndim: (0,) * n) for a in extra]
    key_specs = [per_head(kp.shape[-1]) if kp.ndim == 4
                 else pl.BlockSpec((1, S, kp.shape[-1]), lambda b, h: (b, 0, 0)) for kp in keys]
    return pl.pallas_call(
        kernel_fn,
        grid=(B, H // hps),
        in_specs=extra_specs + [per_head(dq)] + key_specs + [per_head(dv)],
        out_specs=pl.BlockSpec((1, S, hps * dv), lambda b, h: (b, 0, h)),
        out_shape=jax.ShapeDtypeStruct((B, S, H * dv), BF16),
        scratch_shapes=[
            pltpu.VMEM((hps, dv, S), BF16),
            pltpu.VMEM((2, n_maps, S, ROW_BLOCK), F32),
            pltpu.VMEM((2, n_maps, 1, ROW_BLOCK), F32),
        ],
        compiler_params=_cparams("parallel", "parallel"),
        name=name,
    )(*extra, q, *keys, v)


def _diff_proj_kernel(x_ref, mod_ref, cos_ref, sin_ref, win_ref, q_ref, k_ref, v_ref, *, scale):
    m = mod_ref[0]
    h = (x_ref[...] * (1.0 + m[1:2]) + m[0:1]).astype(BF16)
    z = _dot(h, win_ref[...])
    cos_t = cos_ref[...]
    sin_t = sin_ref[...]
    lane = lax.broadcasted_iota(jnp.int32, cos_t.shape, 1)
    low = lane < DIFF_HEAD_DIM
    r = lane % DIFF_HEAD_DIM
    hr = DIFF_ROT // 2
    c = jnp.where(r < DIFF_ROT, jnp.where(low, pltpu.roll(cos_t, DIFF_HEAD_DIM, 1), cos_t), 1.0)
    s = jnp.where(low, pltpu.roll(sin_t, DIFF_HEAD_DIM, 1), sin_t)
    sa = jnp.where(r < hr, -s, 0.0)
    sb = jnp.where((r >= hr) & (r < DIFF_ROT), s, 0.0)
    hw = DIFF_HEADS * LANES

    def rope(t):
        return t * c + pltpu.roll(t, LANES - DIFF_ROT // 2, 1) * sa + pltpu.roll(t, DIFF_ROT // 2, 1) * sb

    for hh in range(DIFF_HEADS):
        a, b = hh * LANES, (hh + 1) * LANES
        q_ref[0, hh] = (rope(z[:, a:b]) * scale).astype(BF16)
        k_ref[0, hh] = rope(z[:, hw + a:hw + b]).astype(BF16)
        v_ref[0, hh] = z[:, 2 * hw + a:2 * hw + b].astype(BF16)


def _diff_proj(x2, mods3, cos_t, sin_t, w_in, B, S):
    T, D = x2.shape
    H = DIFF_HEADS
    tm = min(TOKEN_TILE, S)
    npb = S // tm
    win = w_in.astype(BF16)
    hd = jax.ShapeDtypeStruct((B, H, S, LANES), BF16)
    hspec = pl.BlockSpec((1, H, tm, LANES), lambda i: (i // npb, 0, i % npb, 0))
    return pl.pallas_call(
        functools.partial(_diff_proj_kernel, scale=DIFF_HEAD_DIM ** -0.5 * LOG2E),
        grid=(T // tm,),
        in_specs=[
            pl.BlockSpec((tm, D), lambda i: (i, 0)),
            pl.BlockSpec((1, 6, D), lambda i: (i // npb, 0, 0)),
            pl.BlockSpec((tm, LANES), lambda i: (i, 0)),
            pl.BlockSpec((tm, LANES), lambda i: (i, 0)),
            pl.BlockSpec(win.shape, lambda i: (0, 0)),
        ],
        out_specs=[hspec, hspec, hspec],
        out_shape=[hd, hd, hd],
        compiler_params=_cparams("parallel"),
        name="diff_proj",
    )(x2, mods3, cos_t, sin_t, win)


def _post_attn_kernel(o_ref, x_ref, mod_ref, wo_ref, g_ref, b_ref, rw_ref, x1_ref, aff_ref):
    m = mod_ref[0]
    w_hi, w_lo = _split_bf16(rw_ref[...])
    rw = jnp.concatenate([w_hi, w_lo], axis=1)
    wo = wo_ref[...].astype(BF16)
    for r0 in range(0, o_ref.shape[0], POST_SUB_ROWS):
        rows = slice(r0, r0 + POST_SUB_ROWS)
        t = _dot(o_ref[rows, :], wo)
        x1 = _layer_norm(ALPHA * x_ref[rows, :] + (1.0 + m[2:3]) * t, g_ref[...], b_ref[...])
        x1_ref[rows, :] = x1
        h2 = x1 * (1.0 + m[4:5]) + m[3:4]
        h_hi, h_lo = _split_bf16(h2)
        hw = _dot(h_hi, rw)
        logits = hw[:, :LANES] + (hw[:, LANES:] + _dot(h_lo, w_hi))
        lt = logits.T[:N_EXPERTS]
        mx = jnp.max(lt, axis=0, keepdims=True)
        ex = jnp.exp(lt - mx)
        aff_ref[0, :, rows] = ex / jnp.sum(ex, axis=0, keepdims=True)


def _post_attn(o2, x2, mods3, w_o, ln_g, ln_b, router_w, B, S):
    T, D = x2.shape
    E = N_EXPERTS
    tm = min(POST_TILE, S)
    npb = S // tm
    rw = jnp.pad(router_w, ((0, 0), (0, LANES - E)))
    return pl.pallas_call(
        _post_attn_kernel,
        grid=(T // tm,),
        in_specs=[
            pl.BlockSpec((tm, D), lambda i: (i, 0)),
            pl.BlockSpec((tm, D), lambda i: (i, 0)),
            pl.BlockSpec((1, 6, D), lambda i: (i // npb, 0, 0)),
            pl.BlockSpec((D, D), lambda i: (0, 0)),
            pl.BlockSpec((1, D), lambda i: (0, 0)),
            pl.BlockSpec((1, D), lambda i: (0, 0)),
            pl.BlockSpec((D, LANES), lambda i: (0, 0)),
        ],
        out_specs=[
            pl.BlockSpec((tm, D), lambda i: (i, 0)),
            pl.BlockSpec((1, E, tm), lambda i: (i // npb, 0, i % npb)),
        ],
        out_shape=[
            jax.ShapeDtypeStruct((T, D), F32),
            jax.ShapeDtypeStruct((B, E, S), F32),
        ],
        compiler_params=_cparams("parallel"),
        name="post_attn",
    )(o2, x2, mods3, w_o, ln_g.reshape(1, D), ln_b.reshape(1, D), rw)


def _topk_kernel(aff_ref, idx_ref, gate_ref, pos_ref, off_ref, *, cap):
    a3 = aff_ref[0]
    E, NC, _ = a3.shape
    R = E * NC

    def count3(mask3):
        c = jnp.sum(jnp.where(mask3, 1.0, 0.0), axis=1, keepdims=True)
        return jnp.sum(c, axis=2, keepdims=True)

    def body(i, bits):
        cand = bits | jnp.left_shift(jnp.int32(1), 30 - i)
        reach = count3(a3 >= lax.bitcast_convert_type(cand, F32))
        return jnp.where(reach >= cap, cand, bits)

    thr = lax.bitcast_convert_type(lax.fori_loop(0, 31, body, jnp.zeros((E, 1, 1), jnp.int32)), F32)
    gt3 = a3 > thr
    eq3 = a3 == thr
    need = cap - count3(gt3)

    li = lax.broadcasted_iota(jnp.int32, (LANES, LANES), 0)
    lj = lax.broadcasted_iota(jnp.int32, (LANES, LANES), 1)
    tri_excl = jnp.where(li < lj, 1.0, 0.0).astype(BF16)
    tri_incl = jnp.where(li <= lj, 1.0, 0.0).astype(BF16)
    ri = lax.broadcasted_iota(jnp.int32, (R, R), 0)
    rj = lax.broadcasted_iota(jnp.int32, (R, R), 1)
    same = (ri // NC) == (rj // NC)
    blk_lower = jnp.where(same & (rj < ri), 1.0, 0.0).astype(BF16)

    def chunk_offsets(mask2):
        tot = jnp.sum(mask2, axis=1, keepdims=True)
        totb = jnp.broadcast_to(tot, (R, LANES)).astype(BF16)
        return _dot(blk_lower, totb), tot

    eq2 = jnp.where(eq3, 1.0, 0.0).reshape(R, LANES)
    eq_off, _ = chunk_offsets(eq2)
    eq_rank = (eq_off + _dot(eq2.astype(BF16), tri_excl)).reshape(E, NC, LANES)
    sel3 = gt3 | (eq3 & (eq_rank < need))
    sel2 = jnp.where(sel3, 1.0, 0.0).reshape(R, LANES)
    off2, tot2 = chunk_offsets(sel2)
    cum2 = _dot(sel2.astype(BF16), tri_incl)
    pos_ref[0] = jnp.where(sel3, (off2 + cum2 - 1.0).reshape(E, NC, LANES), -1.0).astype(jnp.int32)
    off_ref[0] = off2.reshape(E, NC, LANES).astype(jnp.int32)
    a2 = a3.reshape(R, LANES)
    a_hi = a2.astype(BF16)
    a_mid = (a2 - a_hi.astype(F32)).astype(BF16)
    a_lo = (a2 - a_hi.astype(F32) - a_mid.astype(F32)).astype(BF16)

    cs = lax.broadcasted_iota(jnp.int32, (NC, cap), 1).astype(F32)
    ks = lax.broadcasted_iota(jnp.int32, (NC, cap), 0).astype(F32)
    ls = lax.broadcasted_iota(jnp.int32, (LANES, cap), 0).astype(F32)
    for e in range(E):
        rows = slice(e * NC, (e + 1) * NC)
        off_e = off2[rows, 0:1]
        end_e = off_e + tot2[rows]
        k_of_c = jnp.sum(jnp.where(end_e <= cs, 1.0, 0.0), axis=0, keepdims=True)
        onehot = ks == k_of_c
        oh = jnp.where(onehot, 1.0, 0.0).astype(BF16)
        off_c = jnp.sum(jnp.where(onehot, off_e, 0.0), axis=0, keepdims=True)
        rank_c = cs[0:1] - off_c
        cum_c = _dot(cum2[rows].T.astype(BF16), oh)
        lane_c = jnp.sum(jnp.where(cum_c <= rank_c, 1.0, 0.0), axis=0, keepdims=True)
        idx_ref[0, e] = (k_of_c * LANES + lane_c).astype(jnp.int32)
        g_c = (_dot(a_hi[rows].T, oh) + _dot(a_mid[rows].T, oh)) + _dot(a_lo[rows].T, oh)
        gate_ref[0, e] = jnp.sum(jnp.where(ls == lane_c, g_c, 0.0), axis=0, keepdims=True)


def _topk(aff, cap):
    B, E, S = aff.shape
    NC = S // LANES
    slot_spec = pl.BlockSpec((1, E, 1, cap), lambda b: (b, 0, 0, 0))
    tok_spec = pl.BlockSpec((1, E, NC, LANES), lambda b: (b, 0, 0, 0))
    return pl.pallas_call(
        functools.partial(_topk_kernel, cap=cap),
        grid=(B,),
        in_specs=[tok_spec],
        out_specs=[slot_spec, slot_spec, tok_spec, tok_spec],
        out_shape=[
            jax.ShapeDtypeStruct((B, E, 1, cap), jnp.int32),
            jax.ShapeDtypeStruct((B, E, 1, cap), F32),
            jax.ShapeDtypeStruct((B, E, NC, LANES), jnp.int32),
            jax.ShapeDtypeStruct((B, E, NC, LANES), jnp.int32),
        ],
        compiler_params=_cparams("parallel"),
        name="ec_topk",
    )(aff.reshape(B, E, NC, LANES))


def _dispatch_kernel(idx_ref, x_ref, mod_ref, xe_ref, rows_ref):
    cap = xe_ref.shape[1]
    m = mod_ref[0]
    scale = 1.0 + m[4:5]
    shift = m[3:4]

    def group(g, carry):
        for r in range(SLOT_ALIGN):
            rows_ref[r:r + 1, :] = x_ref[pl.ds(idx_ref[0, 0, g * SLOT_ALIGN + r], 1), :]
        start = pl.multiple_of(g * SLOT_ALIGN, SLOT_ALIGN)
        xe_ref[0, pl.ds(start, SLOT_ALIGN), :] = (rows_ref[...] * scale + shift).astype(BF16)
        return carry

    lax.fori_loop(0, cap // SLOT_ALIGN, group, 0)


def _dispatch(idx, x1, mods3, B, S, cap):
    E = idx.shape[1]
    D = x1.shape[1]
    return pl.pallas_call(
        _dispatch_kernel,
        grid=(B, E),
        in_specs=[
            pl.BlockSpec((1, 1, cap), lambda b, e: (b * E + e, 0, 0), memory_space=pltpu.SMEM),
            pl.BlockSpec((S, D), lambda b, e: (b, 0)),
            pl.BlockSpec((1, 6, D), lambda b, e: (b, 0, 0)),
        ],
        out_specs=pl.BlockSpec((1, cap, D), lambda b, e: (e, b, 0)),
        out_shape=jax.ShapeDtypeStruct((E, B * cap, D), BF16),
        scratch_shapes=[pltpu.VMEM((SLOT_ALIGN, D), F32)],
        compiler_params=_cparams("parallel", "parallel"),
        name="moe_dispatch",
    )(idx.reshape(B * E, 1, cap), x1, mods3)


def _moe_ffn_kernel(xe_ref, gate_ref, wg_ref, wu_ref, wo_ref, y_ref, act_ref, *, n_hidden_steps):
    s = pl.program_id(1)
    tf = wg_ref.shape[3]

    for f in range(n_hidden_steps):
        @pl.when(s == f)
        def _():
            wg = wg_ref[0, 0].astype(BF16)
            wu = wu_ref[0, 0].astype(BF16)
            for r0 in range(0, xe_ref.shape[1], FFN_SUB_ROWS):
                rows = slice(r0, r0 + FFN_SUB_ROWS)
                xe = xe_ref[0, rows, :]
                g = _dot(xe, wg)
                u = _dot(xe, wu)
                act_ref[rows, f * tf:(f + 1) * tf] = (g * (1.0 / (1.0 + jnp.exp(-g))) * u).astype(BF16)

    @pl.when(s >= n_hidden_steps)
    def _():
        out = _dot(act_ref[...], wo_ref[0, 0].astype(BF16))
        diag = (lax.broadcasted_iota(jnp.int32, (LANES, LANES), 0)
                == lax.broadcasted_iota(jnp.int32, (LANES, LANES), 1))
        for r0 in range(0, out.shape[0], LANES):
            col = jnp.sum(jnp.where(diag, gate_ref[0, :, r0:r0 + LANES], 0.0), axis=1, keepdims=True)
            y_ref[0, r0:r0 + LANES, :] = (out[r0:r0 + LANES, :] * col).astype(BF16)


def _moe_ffn(xe, gate_row, w_in, w_out, layer):
    E, M, D = xe.shape
    FF = w_out.shape[2]
    tf = FF_TILE
    nf = FF // tf
    tn = FFN_OUT_TILE
    nn = D // tn
    hid = lambda s: jnp.minimum(s, nf - 1)
    col = lambda s: jnp.maximum(s - nf, 0)
    return pl.pallas_call(
        functools.partial(_moe_ffn_kernel, n_hidden_steps=nf),
        grid=(E, nf + nn),
        in_specs=[
            pl.BlockSpec((1, M, D), lambda e, s: (e, 0, 0)),
            pl.BlockSpec((1, 1, M), lambda e, s: (e, 0, 0)),
            pl.BlockSpec((1, 1, D, tf), lambda e, s: (layer, e, 0, hid(s))),
            pl.BlockSpec((1, 1, D, tf), lambda e, s: (layer, e, 0, nf + hid(s))),
            pl.BlockSpec((1, 1, FF, tn), lambda e, s: (layer, e, 0, col(s))),
        ],
        out_specs=pl.BlockSpec((1, M, tn), lambda e, s: (e, 0, col(s))),
        out_shape=jax.ShapeDtypeStruct((E, M, D), BF16),
        scratch_shapes=[pltpu.VMEM((M, FF), BF16)],
        compiler_params=_cparams("parallel", "arbitrary"),
        name="moe_ffn",
    )(xe, gate_row, w_in, w_in, w_out)


def _combine_ln_kernel(win_ref, pos_ref, y_ref, x_ref, mod_ref, g_ref, b_ref, o_ref, *, window):
    b = pl.program_id(0)
    j = pl.program_id(1)
    n_exp = y_ref.shape[0]
    tiles = pos_ref.shape[1] // LANES
    slot = lax.broadcasted_iota(jnp.int32, (LANES, window), 1)
    m = mod_ref[0]
    for t in range(tiles):
        rows = slice(t * LANES, (t + 1) * LANES)
        pos = pos_ref[0, rows, :]
        tile = (b * pl.num_programs(1) + j) * tiles + t
        f = None
        for e in range(n_exp):
            start = pl.multiple_of(win_ref[tile * n_exp + e], SLOT_ALIGN)
            onehot = jnp.where(pos[:, e:e + 1] - start == slot, 1.0, 0.0).astype(BF16)
            part = _dot(onehot, y_ref[e, pl.ds(start, window), :])
            f = part if f is None else f + part
        o_ref[rows, :] = _layer_norm(ALPHA * x_ref[rows, :] + (1.0 + m[5:6]) * f, g_ref[...], b_ref[...])


def _combine_ln(pos_t, win, y, x1, mods3, ln_g, ln_b, B, S, cap, window):
    E, _, D = y.shape
    tm = min(COMBINE_TILE, S)
    nj = S // tm
    row = pl.BlockSpec((tm, D), lambda b, j, w: (b * nj + j, 0))
    vec = pl.BlockSpec((1, D), lambda b, j, w: (0, 0))
    return pl.pallas_call(
        functools.partial(_combine_ln_kernel, window=window),
        grid_spec=pltpu.PrefetchScalarGridSpec(
            num_scalar_prefetch=1,
            grid=(B, nj),
            in_specs=[
                pl.BlockSpec((1, tm, E), lambda b, j, w: (b, j, 0)),
                pl.BlockSpec((E, cap, D), lambda b, j, w: (0, b, 0)),
                row,
                pl.BlockSpec((1, 6, D), lambda b, j, w: (b, 0, 0)),
                vec,
                vec,
            ],
            out_specs=row,
        ),
        out_shape=jax.ShapeDtypeStruct((B * S, D), F32),
        compiler_params=_cparams("parallel", "arbitrary"),
        name="moe_combine_ln",
    )(win, pos_t, y, x1, mods3, ln_g.reshape(1, D), ln_b.reshape(1, D))


def _rope_tables(positions):
    B, S = positions.shape
    pos = positions.astype(F32).reshape(B * S, 1)
    inv_m = MLA_THETA ** (-jnp.arange(0, MLA_ROPE, 2, dtype=F32) / MLA_ROPE)
    inv_d = ROPE_THETA ** (-jnp.arange(0, DIFF_ROT, 2, dtype=F32) / DIFF_ROT)
    rest = jnp.zeros((LANES - MLA_ROPE - DIFF_ROT,), F32)
    ang = pos * jnp.concatenate([inv_m, inv_m, inv_d, inv_d, rest])
    return jnp.cos(ang), jnp.sin(ang)


def _moe_block(x1, aff, mods3, w_in, w_out, layer, ln_g, ln_b, B, S):
    E = N_EXPERTS
    cap = max(1, EC_CAPACITY_FACTOR * S // E)
    idx, gate, pos, off = _topk(aff, cap)
    gate_row = gate[:, :, 0, :].transpose(1, 0, 2).reshape(E, 1, B * cap)
    xe = _dispatch(idx, x1, mods3, B, S, cap)
    y = _moe_ffn(xe, gate_row, w_in, w_out, layer)
    window = min(COMBINE_WINDOW, cap)
    pos_t = pos.reshape(B, E, S).transpose(0, 2, 1)
    win = jnp.minimum(off[:, :, :, 0] // SLOT_ALIGN * SLOT_ALIGN, cap - window)
    win = win.transpose(0, 2, 1).reshape(-1)
    return _combine_ln(pos_t, win, y, x1, mods3, ln_g, ln_b, B, S, cap, window)


def kernel(x, c, positions, ada_w, ada_b, ln1_g, ln1_b, ln2_g, ln2_b, mla_w_in, mla_q_norm, mla_kv_norm,
           mla_w_uq, mla_w_ukv, mla_w_o, diff_w_in, diff_lambda, diff_subln, diff_w_o, router_w, moe_w_in,
           moe_w_out):
    B, S, D = x.shape
    cos_t, sin_t = _rope_tables(positions)
    mods = _mods(c, ada_w, ada_b)
    x2 = x.reshape(B * S, D)
    for i in range(DEPTH):
        mods3 = mods[i, :B].reshape(B, 6, D)
        j = i // 2
        if i % 2 == 0:
            q, kn, kpe, v = _mla_proj(x2, mods3, cos_t, sin_t, mla_w_in[j], mla_q_norm[j], mla_kv_norm[j],
                                      mla_w_uq[j], mla_w_ukv[j], B, S)
            o = _attention(_mla_attn_kernel, [], 1, q, [kn, kpe], v, "mla_attn")
            w_o = mla_w_o[j]
        else:
            lambda_init = 0.8 - 0.6 * math.exp(-0.3 * i)
            q, k, v = _diff_proj(x2, mods3, cos_t, sin_t, diff_w_in[j], B, S)
            o = _attention(functools.partial(_diff_attn_kernel, lambda_init=lambda_init),
                           [diff_lambda[j], diff_subln[j].reshape(DIFF_V, 1)], 2, q, [k], v, "diff_attn")
            w_o = diff_w_o[j]
        x1, aff = _post_attn(o.reshape(B * S, D), x2, mods3, w_o, ln1_g[i], ln1_b[i], router_w[i], B, S)
        x2 = _moe_block(x1, aff, mods3, moe_w_in, moe_w_out, i, ln2_g[i], ln2_b[i], B, S)
    return x2.reshape(B, S, D)
```
